```python
import math
import jax, jax.numpy as jnp
from jax import lax
import numpy as np

D_MODEL = 1024
BATCH = 16
SEQ = 4096
DEPTH = 4

GRID_W = 64
CHUNK = 128
Q_BLOCK = 128
D_A = D_MODEL // 2
H_A = 8
HD_A = D_A // H_A
D_B = D_MODEL // 2
HEAD_DIM = 64
H_B = D_B // HEAD_DIM
KV_HEADS = 2
Q_PER_KV = H_B // KV_HEADS
D_KV = KV_HEADS * HEAD_DIM
ROPE_AXIS_DIM = HEAD_DIM // 2
ROPE_THETA = 10000.0
EPS = 1e-6
SPLITS = (D_A, D_A, D_A, D_B, D_KV, D_KV, D_B)
D_IN = sum(SPLITS)
D_MIX = D_A + D_B

kernel_name = "hybrid_gmlp_axial_gqa_encoder"


def rmsnorm(x, g):
    xf = x.astype(jnp.float32)
    y = xf * lax.rsqrt(jnp.mean(xf * xf, axis=-1, keepdims=True) + EPS)
    return (y * g.astype(jnp.float32)).astype(x.dtype)


def layernorm(x, g, b):
    xf = x.astype(jnp.float32)
    mu = jnp.mean(xf, axis=-1, keepdims=True)
    var = jnp.mean(jnp.square(xf - mu), axis=-1, keepdims=True)
    y = (xf - mu) * lax.rsqrt(var + EPS)
    return (y * g.astype(jnp.float32) + b.astype(jnp.float32)).astype(x.dtype)


def axial_rope_tables(seq_len):
    rows = seq_len // GRID_W
    row = jnp.broadcast_to(jnp.arange(rows, dtype=jnp.float32)[:, None], (rows, GRID_W)).reshape(-1)
    col = jnp.broadcast_to(jnp.arange(GRID_W, dtype=jnp.float32)[None, :], (rows, GRID_W)).reshape(-1)
    inv = ROPE_THETA ** (-jnp.arange(0, ROPE_AXIS_DIM, 2, dtype=jnp.float32) / ROPE_AXIS_DIM)
    ang_r = row[:, None] * inv[None, :]
    ang_c = col[:, None] * inv[None, :]
    return jnp.cos(ang_r), jnp.sin(ang_r), jnp.cos(ang_c), jnp.sin(ang_c)


def _rot_half(x, cos, sin):
    n = x.shape[-1] // 2
    c = cos[:, None, :].astype(x.dtype)
    s = sin[:, None, :].astype(x.dtype)
    x1, x2 = x[..., :n], x[..., n:]
    return jnp.concatenate([x1 * c - x2 * s, x2 * c + x1 * s], axis=-1)


def apply_axial_rope(x, tables):
    cr, sr, cc, sc = tables
    xr = _rot_half(x[..., :ROPE_AXIS_DIM], cr, sr)
    xc = _rot_half(x[..., ROPE_AXIS_DIM:], cc, sc)
    return jnp.concatenate([xr, xc], axis=-1)


def gmlp_group(u, v, ln_g, ln_b, ws, sb):
    b, s, _ = v.shape
    nc = s // CHUNK
    vn = layernorm(v, ln_g, ln_b).reshape(b, nc, CHUNK, H_A, HD_A)
    z = jnp.einsum('hij,bcjhd->bcihd', ws, vn) + sb.T[None, None, :, :, None]
    return u * z.reshape(b, s, D_A)


def gqa_axial_attention(q, k, v, qn_g, kn_g, tables):
    b, s, _ = q.shape
    q = rmsnorm(q.reshape(b, s, H_B, HEAD_DIM), qn_g)
    k = rmsnorm(k.reshape(b, s, KV_HEADS, HEAD_DIM), kn_g)
    v = v.reshape(b, s, KV_HEADS, HEAD_DIM)
    q = apply_axial_rope(q, tables)
    k = apply_axial_rope(k, tables)
    scale = HEAD_DIM ** -0.5
    nb = s // Q_BLOCK
    qb = q.reshape(b, nb, Q_BLOCK, KV_HEADS, Q_PER_KV, HEAD_DIM).transpose(1, 0, 2, 3, 4, 5)

    def attend(qblk):
        sc = jnp.einsum('bqkgd,bskd->bkgqs', qblk, k).astype(jnp.float32) * scale
        p = jax.nn.softmax(sc, axis=-1).astype(v.dtype)
        return jnp.einsum('bkgqs,bskd->bqkgd', p, v)

    o = lax.map(attend, qb)
    return o.transpose(1, 0, 2, 3, 4, 5).reshape(b, s, D_B)


def setup_inputs(seed: int = 0) -> dict:
    key = jax.random.key(seed)
    ks = jax.random.split(key, 12)
    f32 = jnp.float32
    x = jax.random.normal(ks[0], (BATCH, SEQ, D_MODEL), f32)
    w_in = jax.random.normal(ks[1], (DEPTH, D_MODEL, D_IN), f32) * D_MODEL ** -0.5
    w_out = jax.random.normal(ks[2], (DEPTH, D_MIX, D_MODEL), f32) * D_MIX ** -0.5
    pre_g = 1.0 + 0.05 * jax.random.normal(ks[3], (DEPTH, D_MODEL), f32)
    post_g = 1.0 + 0.05 * jax.random.normal(ks[4], (DEPTH, D_MODEL), f32)
    ln_a_g = 1.0 + 0.05 * jax.random.normal(ks[5], (DEPTH, D_A), f32)
    ln_a_b = 0.02 * jax.random.normal(ks[6], (DEPTH, D_A), f32)
    spatial_w = jax.random.normal(ks[7], (DEPTH, H_A, CHUNK, CHUNK), f32) * CHUNK ** -0.5
    spatial_b = 1.0 + 0.05 * jax.random.normal(ks[8], (DEPTH, H_A, CHUNK), f32)
    q_norm_g = 1.0 + 0.05 * jax.random.normal(ks[9], (DEPTH, HEAD_DIM), f32)
    k_norm_g = 1.0 + 0.05 * jax.random.normal(ks[10], (DEPTH, HEAD_DIM), f32)
    return {"x": x, "w_in": w_in, "w_out": w_out, "pre_g": pre_g, "post_g": post_g,
            "ln_a_g": ln_a_g, "ln_a_b": ln_a_b, "spatial_w": spatial_w, "spatial_b": spatial_b,
            "q_norm_g": q_norm_g, "k_norm_g": k_norm_g}


def reference(x, w_in, w_out, pre_g, post_g, ln_a_g, ln_a_b, spatial_w, spatial_b, q_norm_g, k_norm_g):
    seq_len = x.shape[1]
    tables = axial_rope_tables(seq_len)
    idx = list(np.cumsum(SPLITS)[:-1])
    for l in range(DEPTH):
        h = rmsnorm(x, pre_g[l])
        proj = jnp.einsum('bsd,de->bse', h, w_in[l])
        u_a, v_a, g_a, q, k, v, g_b = jnp.split(proj, idx, axis=-1)
        y_a = gmlp_group(u_a, v_a, ln_a_g[l], ln_a_b[l], spatial_w[l], spatial_b[l]) * jax.nn.silu(g_a)
        y_b = gqa_axial_attention(q, k, v, q_norm_g[l], k_norm_g[l], tables) * jax.nn.silu(g_b)
        y = jnp.concatenate([y_a, y_b], axis=-1)
        out = jnp.einsum('bse,ed->bsd', y, w_out[l])
        x = x + rmsnorm(out, post_g[l])
    return x
```

```python
import functools
import math

import jax
import jax.numpy as jnp
from jax import lax
from jax.experimental import pallas as pl
from jax.experimental.pallas import tpu as pltpu

GRID_W = 64
CHUNK = 128
Q_BLOCK = 128
H_A = 8
HEAD_DIM = 64
KV_HEADS = 2
ROPE_AXIS_DIM = HEAD_DIM // 2
ROPE_HALF = ROPE_AXIS_DIM // 2
ROPE_THETA = 10000.0
EPS = 1e-6

V_ROWS = HEAD_DIM + 16
ROW_TILE = 512
VMEM_LIMIT_BYTES = 48 * 1024 * 1024

_BF16 = jnp.bfloat16
_F32 = jnp.float32


def _silu(g):
    return g / (1.0 + jnp.exp(-g))


def _rope_t(x, cos, sin):
    n = ROPE_HALF
    a, b, c, d = x[:, 0:n], x[:, n:2 * n], x[:, 2 * n:3 * n], x[:, 3 * n:4 * n]
    cr, cc = cos[0:n][None], cos[n:2 * n][None]
    sr, sc = sin[0:n][None], sin[n:2 * n][None]
    return jnp.concatenate(
        [a * cr - b * sr, b * cr + a * sr, c * cc - d * sc, d * cc + c * sc], axis=1)


def _head_rmsnorm_t(x, g):
    r = lax.rsqrt(jnp.mean(x * x, axis=1, keepdims=True) + EPS)
    return x * r * g[None]


def _in_proj_kernel(x_ref, preg_ref, wnat_ref, wt_ref, lng_ref, lnb_ref, ws2_ref, sbb_ref,
                    qg_ref, kg_ref, cos_ref, sin_ref,
                    ya_ref, sgb_ref, qt_ref, k_ref, vt_ref, *, q_scale):
    tm, _ = x_ref.shape
    d_a = lng_ref.shape[1]
    n_q_heads = qt_ref.shape[1] * (qt_ref.shape[4] // Q_BLOCK)
    q_per_kv = n_q_heads // KV_HEADS
    d_b = n_q_heads * HEAD_DIM
    d_kv = KV_HEADS * HEAD_DIM

    x = x_ref[...]
    r = lax.rsqrt(jnp.mean(x * x, axis=-1, keepdims=True) + EPS)
    h = (x * r * preg_ref[...]).astype(_BF16)

    pn = jnp.dot(h, wnat_ref[...], preferred_element_type=_F32)
    u = pn[:, 0:d_a]
    v = pn[:, d_a:2 * d_a]
    ga = pn[:, 2 * d_a:3 * d_a]
    gb = pn[:, 3 * d_a:3 * d_a + d_b]
    sgb_ref[...] = _silu(gb)

    mu = jnp.mean(v, axis=-1, keepdims=True)
    vc = v - mu
    var = jnp.mean(vc * vc, axis=-1, keepdims=True)
    vn = (vc * lax.rsqrt(var + EPS) * lng_ref[...] + lnb_ref[...]).astype(_BF16)

    lane = lax.broadcasted_iota(jnp.int32, (CHUNK, 2 * HEAD_DIM), 1)
    first = lane < HEAD_DIM
    zero = jnp.zeros((CHUNK, 2 * HEAD_DIM), _BF16)
    z_rows = []
    for c in range(tm // CHUNK):
        z_blocks = []
        for p in range(d_a // (2 * HEAD_DIM)):
            blk = vn[c * CHUNK:(c + 1) * CHUNK, p * 128:(p + 1) * 128]
            rhs = jnp.concatenate([jnp.where(first, blk, zero), jnp.where(first, zero, blk)], axis=0)
            z_blocks.append(jnp.dot(ws2_ref[p], rhs, preferred_element_type=_F32))
        z_rows.append(jnp.concatenate(z_blocks, axis=1) + sbb_ref[...])
    z = jnp.concatenate(z_rows, axis=0)
    ya_ref[...] = (u * z * _silu(ga)).astype(_BF16)

    pt = lax.dot_general(wt_ref[...], h, (((1,), (1,)), ((), ())), preferred_element_type=_F32)
    cos = cos_ref[...]
    sin = sin_ref[...]
    reps = tm // qg_ref.shape[1]
    qg = jnp.concatenate([qg_ref[...]] * reps, axis=1)
    kg = jnp.concatenate([kg_ref[...]] * reps, axis=1)

    qt = pt[0:d_b].reshape(n_q_heads, HEAD_DIM, tm)
    qt = (_rope_t(_head_rmsnorm_t(qt, qg), cos, sin) * q_scale).astype(_BF16)
    for kvh in range(KV_HEADS):
        for jj in range(tm // Q_BLOCK):
            qt_ref[0, kvh, jj] = jnp.concatenate(
                [qt[kvh * q_per_kv + g][:, jj * Q_BLOCK:(jj + 1) * Q_BLOCK] for g in range(q_per_kv)],
                axis=1)

    kt = pt[d_b:d_b + d_kv].reshape(KV_HEADS, HEAD_DIM, tm)
    kt = _rope_t(_head_rmsnorm_t(kt, kg), cos, sin).reshape(d_kv, tm)
    k_ref[...] = kt.T.astype(_BF16)

    vt = pt[d_b + d_kv:d_b + 2 * d_kv].astype(_BF16)
    row = lax.broadcasted_iota(jnp.int32, (V_ROWS - HEAD_DIM, tm), 0)
    ones_rows = jnp.where(row == 0, 1.0, 0.0).astype(_BF16)
    for kvh in range(KV_HEADS):
        vt_ref[0, kvh] = jnp.concatenate([vt[kvh * HEAD_DIM:(kvh + 1) * HEAD_DIM], ones_rows], axis=0)


def _attention_kernel(qt_ref, k_ref, vt_ref, sgb_ref, yb_ref):
    d_pair = 2 * HEAD_DIM
    zero = jnp.zeros((HEAD_DIM, qt_ref.shape[4]), _BF16)
    k = k_ref[...]
    for kvh in range(KV_HEADS):
        qt = qt_ref[0, kvh, 0]
        w = jnp.concatenate([zero] * kvh + [qt] + [zero] * (KV_HEADS - 1 - kvh), axis=0)
        s = jnp.dot(k, w, preferred_element_type=_F32)
        m = jnp.max(s, axis=0, keepdims=True)
        p = jnp.exp2(s - m).astype(_BF16)
        acc = jnp.dot(vt_ref[0, kvh], p, preferred_element_type=_F32)
        o = acc[0:HEAD_DIM] / acc[HEAD_DIM:HEAD_DIM + 1]
        n_pairs = o.shape[1] // (2 * Q_BLOCK)
        for pr in range(n_pairs):
            pair = jnp.concatenate(
                [o[:, (2 * pr) * Q_BLOCK:(2 * pr + 1) * Q_BLOCK],
                 o[:, (2 * pr + 1) * Q_BLOCK:(2 * pr + 2) * Q_BLOCK]], axis=0)
            lo = (kvh * n_pairs + pr) * d_pair
            yb_ref[:, lo:lo + d_pair] = (pair.T * sgb_ref[:, lo:lo + d_pair]).astype(_BF16)


def _out_proj_kernel(x_ref, ya_ref, yb_ref, wo_ref, postg_ref, o_ref):
    y = jnp.concatenate([ya_ref[...], yb_ref[...]], axis=1)
    out = jnp.dot(y, wo_ref[...], preferred_element_type=_F32)
    r = lax.rsqrt(jnp.mean(out * out, axis=-1, keepdims=True) + EPS)
    o_ref[...] = x_ref[...] + out * r * postg_ref[...]


def _rope_tables_t(seq_len):
    pos = jnp.arange(seq_len, dtype=jnp.int32)
    row = (pos // GRID_W).astype(_F32)
    col = (pos % GRID_W).astype(_F32)
    inv = ROPE_THETA ** (-jnp.arange(0, ROPE_AXIS_DIM, 2, dtype=_F32) / ROPE_AXIS_DIM)
    ang = jnp.concatenate([inv[:, None] * row[None, :], inv[:, None] * col[None, :]], axis=0)
    return jnp.cos(ang), jnp.sin(ang)


def _const_spec(shape):
    return pl.BlockSpec(shape, lambda *_: (0,) * len(shape))


class _Dims:
    def __init__(self, x, w_in, ln_a_g):
        self.batch, self.seq, self.d_model = x.shape
        self.d_a = ln_a_g.shape[1]
        self.d_kv = KV_HEADS * HEAD_DIM
        self.d_b = (w_in.shape[2] - 3 * self.d_a - 2 * self.d_kv) // 2
        self.n_q_heads = self.d_b // HEAD_DIM
        self.q_per_kv = self.n_q_heads // KV_HEADS
        self.rows = self.batch * self.seq
        self.tm = min(ROW_TILE, self.seq)
        assert self.seq % self.tm == 0 and self.tm % CHUNK == 0 and self.seq % GRID_W == 0
        assert self.d_a == H_A * HEAD_DIM and self.q_per_kv % 2 == 0
        self.n_qb = self.seq // Q_BLOCK
        self.qw = self.q_per_kv * Q_BLOCK


def _make_calls(dm):
    batch, seq, d_model, d_a, d_b, d_kv = dm.batch, dm.seq, dm.d_model, dm.d_a, dm.d_b, dm.d_kv
    rows, tm, n_qb, qw = dm.rows, dm.tm, dm.n_qb, dm.qw
    d_mix = d_a + d_b
    tiles_per_seq = seq // tm
    qb_per_tile = tm // Q_BLOCK
    q_scale = HEAD_DIM ** -0.5 * math.log2(math.e)
    cparams = pltpu.CompilerParams(dimension_semantics=("arbitrary",), vmem_limit_bytes=VMEM_LIMIT_BYTES)

    in_proj = pl.pallas_call(
        functools.partial(_in_proj_kernel, q_scale=q_scale),
        grid=(rows // tm,),
        in_specs=[
            pl.BlockSpec((tm, d_model), lambda i: (i, 0)),
            _const_spec((1, d_model)),
            _const_spec((d_model, 3 * d_a + d_b)),
            _const_spec((d_b + 2 * d_kv, d_model)),
            _const_spec((1, d_a)),
            _const_spec((1, d_a)),
            _const_spec((H_A // 2, CHUNK, 2 * CHUNK)),
            _const_spec((CHUNK, d_a)),
            _const_spec((HEAD_DIM, 128)),
            _const_spec((HEAD_DIM, 128)),
            pl.BlockSpec((ROPE_AXIS_DIM, tm), lambda i: (0, i % tiles_per_seq)),
            pl.BlockSpec((ROPE_AXIS_DIM, tm), lambda i: (0, i % tiles_per_seq)),
        ],
        out_specs=[
            pl.BlockSpec((tm, d_a), lambda i: (i, 0)),
            pl.BlockSpec((tm, d_b), lambda i: (i, 0)),
            pl.BlockSpec((1, KV_HEADS, qb_per_tile, HEAD_DIM, qw),
                         lambda i: (i // tiles_per_seq, 0, i % tiles_per_seq, 0, 0)),
            pl.BlockSpec((tm, d_kv), lambda i: (i, 0)),
            pl.BlockSpec((1, KV_HEADS, V_ROWS, tm), lambda i: (i // tiles_per_seq, 0, 0, i % tiles_per_seq)),
        ],
        out_shape=[
            jax.ShapeDtypeStruct((rows, d_a), _BF16),
            jax.ShapeDtypeStruct((rows, d_b), _F32),
            jax.ShapeDtypeStruct((batch, KV_HEADS, n_qb, HEAD_DIM, qw), _BF16),
            jax.ShapeDtypeStruct((rows, d_kv), _BF16),
            jax.ShapeDtypeStruct((batch, KV_HEADS, V_ROWS, seq), _BF16),
        ],
        compiler_params=cparams,
        name="in_proj",
    )

    attention = pl.pallas_call(
        _attention_kernel,
        grid=(batch, n_qb),
        in_specs=[
            pl.BlockSpec((1, KV_HEADS, 1, HEAD_DIM, qw), lambda b, j: (b, 0, j, 0, 0)),
            pl.BlockSpec((seq, d_kv), lambda b, j: (b, 0)),
            pl.BlockSpec((1, KV_HEADS, V_ROWS, seq), lambda b, j: (b, 0, 0, 0)),
            pl.BlockSpec((Q_BLOCK, d_b), lambda b, j: (b * n_qb + j, 0)),
        ],
        out_specs=pl.BlockSpec((Q_BLOCK, d_b), lambda b, j: (b * n_qb + j, 0)),
        out_shape=jax.ShapeDtypeStruct((rows, d_b), _BF16),
        compiler_params=pltpu.CompilerParams(
            dimension_semantics=("arbitrary", "arbitrary"), vmem_limit_bytes=VMEM_LIMIT_BYTES),
        name="attention",
    )

    out_proj = pl.pallas_call(
        _out_proj_kernel,
        grid=(rows // tm,),
        in_specs=[
            pl.BlockSpec((tm, d_model), lambda i: (i, 0)),
            pl.BlockSpec((tm, d_a), lambda i: (i, 0)),
            pl.BlockSpec((tm, d_b), lambda i: (i, 0)),
            _const_spec((d_mix, d_model)),
            _const_spec((1, d_model)),
        ],
        out_specs=pl.BlockSpec((tm, d_model), lambda i: (i, 0)),
        out_shape=jax.ShapeDtypeStruct((rows, d_model), _F32),
        compiler_params=cparams,
        name="out_proj",
    )
    return in_proj, attention, out_proj


def _in_proj_operands(dm, wl, pre_g, ln_g, ln_b, spatial_w, spatial_b, q_g, k_g):
    d_a, d_b, d_kv = dm.d_a, dm.d_b, dm.d_kv
    o_u, o_v, o_g, o_q = 0, d_a, 2 * d_a, 3 * d_a
    o_k, o_vv, o_gb = o_q + d_b, o_q + d_b + d_kv, o_q + d_b + 2 * d_kv
    w_nat = jnp.concatenate(
        [wl[:, o_u:o_u + d_a], wl[:, o_v:o_v + d_a], wl[:, o_g:o_g + d_a], wl[:, o_gb:o_gb + d_b]],
        axis=1).astype(_BF16)
    w_t = jnp.concatenate(
        [wl[:, o_q:o_q + d_b], wl[:, o_k:o_k + d_kv], wl[:, o_vv:o_vv + d_kv]], axis=1).T.astype(_BF16)
    ws = spatial_w.astype(_BF16)
    ws2 = jnp.concatenate([ws[0::2], ws[1::2]], axis=2)
    sbb = jnp.repeat(spatial_b.T, HEAD_DIM, axis=1)
    qg = jnp.broadcast_to(q_g[:, None], (HEAD_DIM, 128))
    kg = jnp.broadcast_to(k_g[:, None], (HEAD_DIM, 128))
    return pre_g[None], w_nat, w_t, ln_g[None], ln_b[None], ws2, sbb, qg, kg


def kernel(x, w_in, w_out, pre_g, post_g, ln_a_g, ln_a_b, spatial_w, spatial_b, q_norm_g, k_norm_g):
    dm = _Dims(x, w_in, ln_a_g)
    in_proj, attention, out_proj = _make_calls(dm)
    cos_t, sin_t = _rope_tables_t(dm.seq)
    x2 = x.reshape(dm.rows, dm.d_model)
    for l in range(w_in.shape[0]):
        ops = _in_proj_operands(dm, w_in[l], pre_g[l], ln_a_g[l], ln_a_b[l], spatial_w[l], spatial_b[l],
                                q_norm_g[l], k_norm_g[l])
        ya, sgb, qt, k, vt = in_proj(x2, *ops, cos_t, sin_t)
        yb = attention(qt, k, vt, sgb)
        x2 = out_proj(x2, ya, yb, w_out[l].astype(_BF16), post_g[l][None])
    return x2.reshape(dm.batch, dm.seq, dm.d_model)
```

```python
import functools
import math

import jax
import jax.numpy as jnp
from jax import lax
from jax.experimental import pallas as pl
from jax.experimental.pallas import tpu as pltpu

GRID_W = 64
CHUNK = 128
Q_BLOCK = 128
H_A = 8
HEAD_DIM = 64
KV_HEADS = 2
ROPE_AXIS_DIM = HEAD_DIM // 2
ROPE_HALF = ROPE_AXIS_DIM // 2
ROPE_THETA = 10000.0
EPS = 1e-6

V_ROWS = HEAD_DIM + 16
ROW_TILE = 512
KV_CHUNK = 256
VMEM_LIMIT_BYTES = 48 * 1024 * 1024

_BF16 = jnp.bfloat16
_F32 = jnp.float32


def _silu(g):
    return g / (1.0 + jnp.exp(-g))


def _rope_t(x, cos, sin):
    n = ROPE_HALF
    a, b, c, d = x[:, 0:n], x[:, n:2 * n], x[:, 2 * n:3 * n], x[:, 3 * n:4 * n]
    cr, cc = cos[0:n][None], cos[n:2 * n][None]
    sr, sc = sin[0:n][None], sin[n:2 * n][None]
    return jnp.concatenate(
        [a * cr - b * sr, b * cr + a * sr, c * cc - d * sc, d * cc + c * sc], axis=1)


def _head_rmsnorm_t(x, g):
    r = lax.rsqrt(jnp.mean(x * x, axis=1, keepdims=True) + EPS)
    return x * r * g[None]


def _in_proj_kernel(x_ref, preg_ref, wnat_ref, wt_ref, lng_ref, lnb_ref, ws2_ref, sbb_ref,
                    qg_ref, kg_ref, cos_ref, sin_ref,
                    ya_ref, sgb_ref, qt_ref, k_ref, vt_ref, *, q_scale):
    tm, _ = x_ref.shape
    d_a = lng_ref.shape[1]
    n_q_heads = qt_ref.shape[1] * (qt_ref.shape[4] // Q_BLOCK)
    q_per_kv = n_q_heads // KV_HEADS
    d_b = n_q_heads * HEAD_DIM
    d_kv = KV_HEADS * HEAD_DIM

    x = x_ref[...]
    r = lax.rsqrt(jnp.mean(x * x, axis=-1, keepdims=True) + EPS)
    h = (x * r * preg_ref[...]).astype(_BF16)

    pn = jnp.dot(h, wnat_ref[...], preferred_element_type=_F32)
    u = pn[:, 0:d_a]
    v = pn[:, d_a:2 * d_a]
    ga = pn[:, 2 * d_a:3 * d_a]
    gb = pn[:, 3 * d_a:3 * d_a + d_b]
    sgb_ref[...] = _silu(gb)

    mu = jnp.mean(v, axis=-1, keepdims=True)
    vc = v - mu
    var = jnp.mean(vc * vc, axis=-1, keepdims=True)
    vn = (vc * lax.rsqrt(var + EPS) * lng_ref[...] + lnb_ref[...]).astype(_BF16)

    lane = lax.broadcasted_iota(jnp.int32, (CHUNK, 2 * HEAD_DIM), 1)
    first = lane < HEAD_DIM
    zero = jnp.zeros((CHUNK, 2 * HEAD_DIM), _BF16)
    z_rows = []
    for c in range(tm // CHUNK):
        z_blocks = []
        for p in range(d_a // (2 * HEAD_DIM)):
            blk = vn[c * CHUNK:(c + 1) * CHUNK, p * 128:(p + 1) * 128]
            rhs = jnp.concatenate([jnp.where(first, blk, zero), jnp.where(first, zero, blk)], axis=0)
            z_blocks.append(jnp.dot(ws2_ref[p], rhs, preferred_element_type=_F32))
        z_rows.append(jnp.concatenate(z_blocks, axis=1) + sbb_ref[...])
    z = jnp.concatenate(z_rows, axis=0)
    ya_ref[...] = (u * z * _silu(ga)).astype(_BF16)

    pt = lax.dot_general(wt_ref[...], h, (((1,), (1,)), ((), ())), preferred_element_type=_F32)
    cos = cos_ref[...]
    sin = sin_ref[...]
    reps = tm // qg_ref.shape[1]
    qg = jnp.concatenate([qg_ref[...]] * reps, axis=1)
    kg = jnp.concatenate([kg_ref[...]] * reps, axis=1)

    qt = pt[0:d_b].reshape(n_q_heads, HEAD_DIM, tm)
    qt = (_rope_t(_head_rmsnorm_t(qt, qg), cos, sin) * q_scale).astype(_BF16)
    for kvh in range(KV_HEADS):
        for jj in range(tm // Q_BLOCK):
            qt_ref[0, kvh, jj] = jnp.concatenate(
                [qt[kvh * q_per_kv + g][:, jj * Q_BLOCK:(jj + 1) * Q_BLOCK] for g in range(q_per_kv)],
                axis=1)

    kt = pt[d_b:d_b + d_kv].reshape(KV_HEADS, HEAD_DIM, tm)
    kt = _rope_t(_head_rmsnorm_t(kt, kg), cos, sin).reshape(d_kv, tm)
    k_ref[...] = kt.T.astype(_BF16)

    vt = pt[d_b + d_kv:d_b + 2 * d_kv].astype(_BF16)
    row = lax.broadcasted_iota(jnp.int32, (V_ROWS - HEAD_DIM, tm), 0)
    ones_rows = jnp.where(row == 0, 1.0, 0.0).astype(_BF16)
    for kvh in range(KV_HEADS):
        vt_ref[0, kvh] = jnp.concatenate([vt[kvh * HEAD_DIM:(kvh + 1) * HEAD_DIM], ones_rows], axis=0)


def _score_weights(qt, kvh):
    zero = jnp.zeros_like(qt)
    return jnp.concatenate([zero] * kvh + [qt] + [zero] * (KV_HEADS - 1 - kvh), axis=0)


def _score_chunk(k_ref, w, s_ref, m8, c):
    rows = slice(c * KV_CHUNK, (c + 1) * KV_CHUNK)
    s = jnp.dot(k_ref[rows, :], w, preferred_element_type=_F32)
    s_ref[rows, :] = s
    cm = jnp.max(s.reshape(KV_CHUNK // 8, 8, s.shape[1]), axis=0)
    return cm if m8 is None else jnp.maximum(m8, cm)


def _value_chunk(s_ref, m, vt_ref, kvh, acc, c):
    rows = slice(c * KV_CHUNK, (c + 1) * KV_CHUNK)
    p = jnp.exp2(s_ref[rows, :] - m).astype(_BF16)
    part = jnp.dot(vt_ref[0, kvh, :, rows], p, preferred_element_type=_F32)
    return part if acc is None else acc + part


def _store_head_output(acc, sgb_ref, yb_ref, kvh):
    d_pair = 2 * HEAD_DIM
    o = acc[0:HEAD_DIM] / acc[HEAD_DIM:HEAD_DIM + 1]
    n_pairs = o.shape[1] // (2 * Q_BLOCK)
    for pr in range(n_pairs):
        pair = jnp.concatenate(
            [o[:, (2 * pr) * Q_BLOCK:(2 * pr + 1) * Q_BLOCK],
             o[:, (2 * pr + 1) * Q_BLOCK:(2 * pr + 2) * Q_BLOCK]], axis=0)
        lo = (kvh * n_pairs + pr) * d_pair
        yb_ref[:, lo:lo + d_pair] = (pair.T * sgb_ref[:, lo:lo + d_pair]).astype(_BF16)


def _scores_and_values(qt, kvh_s, s_w_ref, m_w_ref, s_r_ref, m_r_ref, kvh_v, k_ref, vt_ref, sgb_ref, yb_ref):
    w = _score_weights(qt, kvh_s)
    m = jnp.max(m_r_ref[...], axis=0, keepdims=True)
    m8 = None
    acc = None
    for c in range(k_ref.shape[0] // KV_CHUNK):
        m8 = _score_chunk(k_ref, w, s_w_ref, m8, c)
        acc = _value_chunk(s_r_ref, m, vt_ref, kvh_v, acc, c)
    m_w_ref[...] = m8
    _store_head_output(acc, sgb_ref, yb_ref, kvh_v)


def _attention_kernel(qt_ref, qt_next_ref, k_ref, vt_ref, sgb_ref, yb_ref, s0_ref, s1_ref, m0_ref, m1_ref):
    @pl.when(pl.program_id(1) == 0)
    def _():
        w = _score_weights(qt_ref[0, 0, 0], 0)
        m8 = None
        for c in range(k_ref.shape[0] // KV_CHUNK):
            m8 = _score_chunk(k_ref, w, s0_ref, m8, c)
        m0_ref[...] = m8

    _scores_and_values(qt_ref[0, 1, 0], 1, s1_ref, m1_ref, s0_ref, m0_ref, 0, k_ref, vt_ref, sgb_ref, yb_ref)

    @pl.when(pl.program_id(1) >= 0)
    def _():
        _scores_and_values(
            qt_next_ref[0, 0, 0], 0, s0_ref, m0_ref, s1_ref, m1_ref, 1, k_ref, vt_ref, sgb_ref, yb_ref)


def _out_proj_kernel(x_ref, ya_ref, yb_ref, wo_ref, postg_ref, o_ref):
    y = jnp.concatenate([ya_ref[...], yb_ref[...]], axis=1)
    out = jnp.dot(y, wo_ref[...], preferred_element_type=_F32)
    r = lax.rsqrt(jnp.mean(out * out, axis=-1, keepdims=True) + EPS)
    o_ref[...] = x_ref[...] + out * r * postg_ref[...]


def _rope_tables_t(seq_len):
    pos = jnp.arange(seq_len, dtype=jnp.int32)
    row = (pos // GRID_W).astype(_F32)
    col = (pos % GRID_W).astype(_F32)
    inv = ROPE_THETA ** (-jnp.arange(0, ROPE_AXIS_DIM, 2, dtype=_F32) / ROPE_AXIS_DIM)
    ang = jnp.concatenate([inv[:, None] * row[None, :], inv[:, None] * col[None, :]], axis=0)
    return jnp.cos(ang), jnp.sin(ang)


def _const_spec(shape):
    return pl.BlockSpec(shape, lambda *_: (0,) * len(shape))


class _Dims:
    def __init__(self, x, w_in, ln_a_g):
        self.batch, self.seq, self.d_model = x.shape
        self.d_a = ln_a_g.shape[1]
        self.d_kv = KV_HEADS * HEAD_DIM
        self.d_b = (w_in.shape[2] - 3 * self.d_a - 2 * self.d_kv) // 2
        self.n_q_heads = self.d_b // HEAD_DIM
        self.q_per_kv = self.n_q_heads // KV_HEADS
        self.rows = self.batch * self.seq
        self.tm = min(ROW_TILE, self.seq)
        assert self.seq % self.tm == 0 and self.tm % CHUNK == 0 and self.seq % GRID_W == 0
        assert self.d_a == H_A * HEAD_DIM and self.q_per_kv % 2 == 0
        self.n_qb = self.seq // Q_BLOCK
        self.qw = self.q_per_kv * Q_BLOCK


def _make_calls(dm):
    batch, seq, d_model, d_a, d_b, d_kv = dm.batch, dm.seq, dm.d_model, dm.d_a, dm.d_b, dm.d_kv
    rows, tm, n_qb, qw = dm.rows, dm.tm, dm.n_qb, dm.qw
    d_mix = d_a + d_b
    tiles_per_seq = seq // tm
    qb_per_tile = tm // Q_BLOCK
    q_scale = HEAD_DIM ** -0.5 * math.log2(math.e)
    cparams = pltpu.CompilerParams(dimension_semantics=("arbitrary",), vmem_limit_bytes=VMEM_LIMIT_BYTES)

    in_proj = pl.pallas_call(
        functools.partial(_in_proj_kernel, q_scale=q_scale),
        grid=(rows // tm,),
        in_specs=[
            pl.BlockSpec((tm, d_model), lambda i: (i, 0)),
            _const_spec((1, d_model)),
            _const_spec((d_model, 3 * d_a + d_b)),
            _const_spec((d_b + 2 * d_kv, d_model)),
            _const_spec((1, d_a)),
            _const_spec((1, d_a)),
            _const_spec((H_A // 2, CHUNK, 2 * CHUNK)),
            _const_spec((CHUNK, d_a)),
            _const_spec((HEAD_DIM, 128)),
            _const_spec((HEAD_DIM, 128)),
            pl.BlockSpec((ROPE_AXIS_DIM, tm), lambda i: (0, i % tiles_per_seq)),
            pl.BlockSpec((ROPE_AXIS_DIM, tm), lambda i: (0, i % tiles_per_seq)),
        ],
        out_specs=[
            pl.BlockSpec((tm, d_a), lambda i: (i, 0)),
            pl.BlockSpec((tm, d_b), lambda i: (i, 0)),
            pl.BlockSpec((1, KV_HEADS, qb_per_tile, HEAD_DIM, qw),
                         lambda i: (i // tiles_per_seq, 0, i % tiles_per_seq, 0, 0)),
            pl.BlockSpec((tm, d_kv), lambda i: (i, 0)),
            pl.BlockSpec((1, KV_HEADS, V_ROWS, tm), lambda i: (i // tiles_per_seq, 0, 0, i % tiles_per_seq)),
        ],
        out_shape=[
            jax.ShapeDtypeStruct((rows, d_a), _BF16),
            jax.ShapeDtypeStruct((rows, d_b), _F32),
            jax.ShapeDtypeStruct((batch, KV_HEADS, n_qb, HEAD_DIM, qw), _BF16),
            jax.ShapeDtypeStruct((rows, d_kv), _BF16),
            jax.ShapeDtypeStruct((batch, KV_HEADS, V_ROWS, seq), _BF16),
        ],
        compiler_params=cparams,
        name="in_proj",
    )

    attention = pl.pallas_call(
        _attention_kernel,
        grid=(batch, n_qb),
        in_specs=[
            pl.BlockSpec((1, KV_HEADS, 1, HEAD_DIM, qw), lambda b, j: (b, 0, j, 0, 0)),
            pl.BlockSpec((1, KV_HEADS, 1, HEAD_DIM, qw), lambda b, j: (b, 0, jnp.minimum(j + 1, n_qb - 1), 0, 0)),
            pl.BlockSpec((seq, d_kv), lambda b, j: (b, 0)),
            pl.BlockSpec((1, KV_HEADS, V_ROWS, seq), lambda b, j: (b, 0, 0, 0)),
            pl.BlockSpec((Q_BLOCK, d_b), lambda b, j: (b * n_qb + j, 0)),
        ],
        out_specs=pl.BlockSpec((Q_BLOCK, d_b), lambda b, j: (b * n_qb + j, 0)),
        out_shape=jax.ShapeDtypeStruct((rows, d_b), _BF16),
        scratch_shapes=[pltpu.VMEM((seq, qw), _F32), pltpu.VMEM((seq, qw), _F32),
                        pltpu.VMEM((8, qw), _F32), pltpu.VMEM((8, qw), _F32)],
        compiler_params=pltpu.CompilerParams(
            dimension_semantics=("arbitrary", "arbitrary"), vmem_limit_bytes=VMEM_LIMIT_BYTES),
        name="attention",
    )

    out_proj = pl.pallas_call(
        _out_proj_kernel,
        grid=(rows // tm,),
        in_specs=[
            pl.BlockSpec((tm, d_model), lambda i: (i, 0)),
            pl.BlockSpec((tm, d_a), lambda i: (i, 0)),
            pl.BlockSpec((tm, d_b), lambda i: (i, 0)),
            _const_spec((d_mix, d_model)),
            _const_spec((1, d_model)),
        ],
        out_specs=pl.BlockSpec((tm, d_model), lambda i: (i, 0)),
        out_shape=jax.ShapeDtypeStruct((rows, d_model), _F32),
        compiler_params=cparams,
        name="out_proj",
    )
    return in_proj, attention, out_proj


def _in_proj_operands(dm, wl, pre_g, ln_g, ln_b, spatial_w, spatial_b, q_g, k_g):
    d_a, d_b, d_kv = dm.d_a, dm.d_b, dm.d_kv
    o_u, o_v, o_g, o_q = 0, d_a, 2 * d_a, 3 * d_a
    o_k, o_vv, o_gb = o_q + d_b, o_q + d_b + d_kv, o_q + d_b + 2 * d_kv
    w_nat = jnp.concatenate(
        [wl[:, o_u:o_u + d_a], wl[:, o_v:o_v + d_a], wl[:, o_g:o_g + d_a], wl[:, o_gb:o_gb + d_b]],
        axis=1).astype(_BF16)
    w_t = jnp.concatenate(
        [wl[:, o_q:o_q + d_b], wl[:, o_k:o_k + d_kv], wl[:, o_vv:o_vv + d_kv]], axis=1).T.astype(_BF16)
    ws = spatial_w.astype(_BF16)
    ws2 = jnp.concatenate([ws[0::2], ws[1::2]], axis=2)
    sbb = jnp.repeat(spatial_b.T, HEAD_DIM, axis=1)
    qg = jnp.broadcast_to(q_g[:, None], (HEAD_DIM, 128))
    kg = jnp.broadcast_to(k_g[:, None], (HEAD_DIM, 128))
    return pre_g[None], w_nat, w_t, ln_g[None], ln_b[None], ws2, sbb, qg, kg


def kernel(x, w_in, w_out, pre_g, post_g, ln_a_g, ln_a_b, spatial_w, spatial_b, q_norm_g, k_norm_g):
    dm = _Dims(x, w_in, ln_a_g)
    in_proj, attention, out_proj = _make_calls(dm)
    cos_t, sin_t = _rope_tables_t(dm.seq)
    x2 = x.reshape(dm.rows, dm.d_model)
    for l in range(w_in.shape[0]):
        ops = _in_proj_operands(dm, w_in[l], pre_g[l], ln_a_g[l], ln_a_b[l], spatial_w[l], spatial_b[l],
                                q_norm_g[l], k_norm_g[l])
        ya, sgb, qt, k, vt = in_proj(x2, *ops, cos_t, sin_t)
        yb = attention(qt, qt, k, vt, sgb)
        x2 = out_proj(x2, ya, yb, w_out[l].astype(_BF16), post_g[l][None])
    return x2.reshape(dm.batch, dm.seq, dm.d_model)
```

```python
import functools
import math

import jax
import jax.numpy as jnp
from jax import lax
from jax.experimental import pallas as pl
from jax.experimental.pallas import tpu as pltpu

GRID_W = 64
CHUNK = 128
Q_BLOCK = 128
H_A = 8
HEAD_DIM = 64
KV_HEADS = 2
ROPE_AXIS_DIM = HEAD_DIM // 2
ROPE_HALF = ROPE_AXIS_DIM // 2
ROPE_THETA = 10000.0
EPS = 1e-6

V_ROWS = HEAD_DIM + 16
ROW_TILE = 512
KV_CHUNK = 256
SCORE_LAG = 2
BLOCKS_PER_STEP = 2
VMEM_LIMIT_BYTES = 56 * 1024 * 1024

_BF16 = jnp.bfloat16
_F32 = jnp.float32


def _silu(g):
    return g / (1.0 + jnp.exp(-g))


def _rope_t(x, cos, sin):
    n = ROPE_HALF
    a, b, c, d = x[:, 0:n], x[:, n:2 * n], x[:, 2 * n:3 * n], x[:, 3 * n:4 * n]
    cr, cc = cos[0:n][None], cos[n:2 * n][None]
    sr, sc = sin[0:n][None], sin[n:2 * n][None]
    return jnp.concatenate(
        [a * cr - b * sr, b * cr + a * sr, c * cc - d * sc, d * cc + c * sc], axis=1)


def _head_rmsnorm_t(x, g):
    r = lax.rsqrt(jnp.mean(x * x, axis=1, keepdims=True) + EPS)
    return x * r * g[None]


def _in_proj_kernel(x_ref, preg_ref, wnat_ref, wt_ref, lng_ref, lnb_ref, ws2_ref, sbb_ref,
                    qg_ref, kg_ref, cos_ref, sin_ref,
                    ya_ref, sgb_ref, qt_ref, k_ref, vt_ref, *, q_scale):
    tm, _ = x_ref.shape
    d_a = lng_ref.shape[1]
    n_q_heads = qt_ref.shape[1] * (qt_ref.shape[4] // Q_BLOCK)
    q_per_kv = n_q_heads // KV_HEADS
    d_b = n_q_heads * HEAD_DIM
    d_kv = KV_HEADS * HEAD_DIM

    x = x_ref[...]
    r = lax.rsqrt(jnp.mean(x * x, axis=-1, keepdims=True) + EPS)
    h = (x * r * preg_ref[...]).astype(_BF16)

    pn = jnp.dot(h, wnat_ref[...], preferred_element_type=_F32)
    u = pn[:, 0:d_a]
    v = pn[:, d_a:2 * d_a]
    ga = pn[:, 2 * d_a:3 * d_a]
    gb = pn[:, 3 * d_a:3 * d_a + d_b]
    sgb_ref[...] = _silu(gb)

    mu = jnp.mean(v, axis=-1, keepdims=True)
    vc = v - mu
    var = jnp.mean(vc * vc, axis=-1, keepdims=True)
    vn = (vc * lax.rsqrt(var + EPS) * lng_ref[...] + lnb_ref[...]).astype(_BF16)

    lane = lax.broadcasted_iota(jnp.int32, (CHUNK, 2 * HEAD_DIM), 1)
    first = lane < HEAD_DIM
    zero = jnp.zeros((CHUNK, 2 * HEAD_DIM), _BF16)
    z_rows = []
    for c in range(tm // CHUNK):
        z_blocks = []
        for p in range(d_a // (2 * HEAD_DIM)):
            blk = vn[c * CHUNK:(c + 1) * CHUNK, p * 128:(p + 1) * 128]
            rhs = jnp.concatenate([jnp.where(first, blk, zero), jnp.where(first, zero, blk)], axis=0)
            z_blocks.append(jnp.dot(ws2_ref[p], rhs, preferred_element_type=_F32))
        z_rows.append(jnp.concatenate(z_blocks, axis=1) + sbb_ref[...])
    z = jnp.concatenate(z_rows, axis=0)
    ya_ref[...] = (u * z * _silu(ga)).astype(_BF16)

    pt = lax.dot_general(wt_ref[...], h, (((1,), (1,)), ((), ())), preferred_element_type=_F32)
    cos = cos_ref[...]
    sin = sin_ref[...]
    reps = tm // qg_ref.shape[1]
    qg = jnp.concatenate([qg_ref[...]] * reps, axis=1)
    kg = jnp.concatenate([kg_ref[...]] * reps, axis=1)

    qt = pt[0:d_b].reshape(n_q_heads, HEAD_DIM, tm)
    qt = (_rope_t(_head_rmsnorm_t(qt, qg), cos, sin) * q_scale).astype(_BF16)
    for kvh in range(KV_HEADS):
        for jj in range(tm // Q_BLOCK):
            qt_ref[0, kvh, jj] = jnp.concatenate(
                [qt[kvh * q_per_kv + g][:, jj * Q_BLOCK:(jj + 1) * Q_BLOCK] for g in range(q_per_kv)],
                axis=1)

    kt = pt[d_b:d_b + d_kv].reshape(KV_HEADS, HEAD_DIM, tm)
    kt = _rope_t(_head_rmsnorm_t(kt, kg), cos, sin).reshape(d_kv, tm)
    k_ref[...] = kt.T.astype(_BF16)

    vt = pt[d_b + d_kv:d_b + 2 * d_kv].astype(_BF16)
    row = lax.broadcasted_iota(jnp.int32, (V_ROWS - HEAD_DIM, tm), 0)
    ones_rows = jnp.where(row == 0, 1.0, 0.0).astype(_BF16)
    for kvh in range(KV_HEADS):
        vt_ref[0, kvh] = jnp.concatenate([vt[kvh * HEAD_DIM:(kvh + 1) * HEAD_DIM], ones_rows], axis=0)


def _score_weights(qt, kvh):
    zero = jnp.zeros_like(qt)
    return jnp.concatenate([zero] * kvh + [qt] + [zero] * (KV_HEADS - 1 - kvh), axis=0)


def _score_chunk(k_ref, w, s_ref, m8, c):
    rows = slice(c * KV_CHUNK, (c + 1) * KV_CHUNK)
    s = jnp.dot(k_ref[rows, :], w, preferred_element_type=_F32)
    s_ref[rows, :] = s
    cm = jnp.max(s.reshape(KV_CHUNK // 8, 8, s.shape[1]), axis=0)
    return cm if m8 is None else jnp.maximum(m8, cm)


def _value_chunk(s_ref, m, vt_ref, kvh, acc, c, row0):
    rows = slice(c * KV_CHUNK, (c + 1) * KV_CHUNK)
    srows = pl.ds(pl.multiple_of(row0 + c * KV_CHUNK, KV_CHUNK), KV_CHUNK)
    p = jnp.exp2(s_ref[srows, :] - m).astype(_BF16)
    part = jnp.dot(vt_ref[0, kvh, :, rows], p, preferred_element_type=_F32)
    return part if acc is None else acc + part


def _store_head_output(acc, sgb_ref, yb_ref, kvh, blk):
    d_pair = 2 * HEAD_DIM
    o = acc[0:HEAD_DIM] / acc[HEAD_DIM:HEAD_DIM + 1]
    n_pairs = o.shape[1] // (2 * Q_BLOCK)
    qrows = slice(blk * Q_BLOCK, (blk + 1) * Q_BLOCK)
    for pr in range(n_pairs):
        pair = jnp.concatenate(
            [o[:, (2 * pr) * Q_BLOCK:(2 * pr + 1) * Q_BLOCK],
             o[:, (2 * pr + 1) * Q_BLOCK:(2 * pr + 2) * Q_BLOCK]], axis=0)
        lo = (kvh * n_pairs + pr) * d_pair
        yb_ref[qrows, lo:lo + d_pair] = (pair.T * sgb_ref[qrows, lo:lo + d_pair]).astype(_BF16)


def _scores_only(qt, kvh, k_ref, s_ref, m_ref):
    w = _score_weights(qt, kvh)
    m8 = None
    for c in range(k_ref.shape[0] // KV_CHUNK):
        m8 = _score_chunk(k_ref, w, s_ref, m8, c)
    m_ref[...] = m8


def _values_and_scores(s_r_ref, m_r_ref, kvh, blk, qt, k_ref, s_w_ref, m_w_ref, vt_ref, sgb_ref, yb_ref, row0):
    n = k_ref.shape[0] // KV_CHUNK
    w = _score_weights(qt, kvh)
    m = jnp.max(m_r_ref[...], axis=0, keepdims=True)
    m8 = None
    acc = None
    for c in range(n):
        acc = _value_chunk(s_r_ref, m, vt_ref, kvh, acc, c, row0)
        if c >= SCORE_LAG:
            m8 = _score_chunk(k_ref, w, s_w_ref, m8, c - SCORE_LAG)
    _store_head_output(acc, sgb_ref, yb_ref, kvh, blk)
    for c in range(n - SCORE_LAG, n):
        m8 = _score_chunk(k_ref, w, s_w_ref, m8, c)
    m_w_ref[...] = m8


def _attention_kernel(qt_ref, qt_next_ref, k_ref, k_next_ref, vt_ref, sgb_ref, yb_ref,
                      s0_ref, s1_ref, s2_ref, s3_ref, m0_ref, m1_ref, m2_ref, m3_ref):
    @pl.when((pl.program_id(0) == 0) & (pl.program_id(1) == 0))
    def _():
        _scores_only(qt_ref[0, 0, 0], 0, k_ref, s0_ref, m0_ref)
        _scores_only(qt_ref[0, 1, 0], 1, k_ref, s1_ref, m1_ref)

    row0 = jnp.minimum(pl.program_id(1), 0)
    common = (vt_ref, sgb_ref, yb_ref, row0)
    _values_and_scores(s0_ref, m0_ref, 0, 0, qt_ref[0, 0, 1], k_ref, s2_ref, m2_ref, *common)
    _values_and_scores(s1_ref, m1_ref, 1, 0, qt_ref[0, 1, 1], k_ref, s3_ref, m3_ref, *common)
    _values_and_scores(s2_ref, m2_ref, 0, 1, qt_next_ref[0, 0, 0], k_next_ref, s0_ref, m0_ref, *common)
    _values_and_scores(s3_ref, m3_ref, 1, 1, qt_next_ref[0, 1, 0], k_next_ref, s1_ref, m1_ref, *common)


def _out_proj_kernel(x_ref, ya_ref, yb_ref, wo_ref, postg_ref, o_ref):
    y = jnp.concatenate([ya_ref[...], yb_ref[...]], axis=1)
    out = jnp.dot(y, wo_ref[...], preferred_element_type=_F32)
    r = lax.rsqrt(jnp.mean(out * out, axis=-1, keepdims=True) + EPS)
    o_ref[...] = x_ref[...] + out * r * postg_ref[...]


def _rope_tables_t(seq_len):
    pos = jnp.arange(seq_len, dtype=jnp.int32)
    row = (pos // GRID_W).astype(_F32)
    col = (pos % GRID_W).astype(_F32)
    inv = ROPE_THETA ** (-jnp.arange(0, ROPE_AXIS_DIM, 2, dtype=_F32) / ROPE_AXIS_DIM)
    ang = jnp.concatenate([inv[:, None] * row[None, :], inv[:, None] * col[None, :]], axis=0)
    return jnp.cos(ang), jnp.sin(ang)


def _const_spec(shape):
    return pl.BlockSpec(shape, lambda *_: (0,) * len(shape))


class _Dims:
    def __init__(self, x, w_in, ln_a_g):
        self.batch, self.seq, self.d_model = x.shape
        self.d_a = ln_a_g.shape[1]
        self.d_kv = KV_HEADS * HEAD_DIM
        self.d_b = (w_in.shape[2] - 3 * self.d_a - 2 * self.d_kv) // 2
        self.n_q_heads = self.d_b // HEAD_DIM
        self.q_per_kv = self.n_q_heads // KV_HEADS
        self.rows = self.batch * self.seq
        self.tm = min(ROW_TILE, self.seq)
        assert self.seq % self.tm == 0 and self.tm % CHUNK == 0 and self.seq % GRID_W == 0
        assert self.d_a == H_A * HEAD_DIM and self.q_per_kv % 2 == 0
        assert (self.seq // Q_BLOCK) % BLOCKS_PER_STEP == 0
        self.n_qb = self.seq // Q_BLOCK
        self.qw = self.q_per_kv * Q_BLOCK


def _make_calls(dm):
    batch, seq, d_model, d_a, d_b, d_kv = dm.batch, dm.seq, dm.d_model, dm.d_a, dm.d_b, dm.d_kv
    rows, tm, n_qb, qw = dm.rows, dm.tm, dm.n_qb, dm.qw
    d_mix = d_a + d_b
    tiles_per_seq = seq // tm
    qb_per_tile = tm // Q_BLOCK
    q_scale = HEAD_DIM ** -0.5 * math.log2(math.e)
    cparams = pltpu.CompilerParams(dimension_semantics=("arbitrary",), vmem_limit_bytes=VMEM_LIMIT_BYTES)

    in_proj = pl.pallas_call(
        functools.partial(_in_proj_kernel, q_scale=q_scale),
        grid=(rows // tm,),
        in_specs=[
            pl.BlockSpec((tm, d_model), lambda i: (i, 0)),
            _const_spec((1, d_model)),
            _const_spec((d_model, 3 * d_a + d_b)),
            _const_spec((d_b + 2 * d_kv, d_model)),
            _const_spec((1, d_a)),
            _const_spec((1, d_a)),
            _const_spec((H_A // 2, CHUNK, 2 * CHUNK)),
            _const_spec((CHUNK, d_a)),
            _const_spec((HEAD_DIM, 128)),
            _const_spec((HEAD_DIM, 128)),
            pl.BlockSpec((ROPE_AXIS_DIM, tm), lambda i: (0, i % tiles_per_seq)),
            pl.BlockSpec((ROPE_AXIS_DIM, tm), lambda i: (0, i % tiles_per_seq)),
        ],
        out_specs=[
            pl.BlockSpec((tm, d_a), lambda i: (i, 0)),
            pl.BlockSpec((tm, d_b), lambda i: (i, 0)),
            pl.BlockSpec((1, KV_HEADS, qb_per_tile, HEAD_DIM, qw),
                         lambda i: (i // tiles_per_seq, 0, i % tiles_per_seq, 0, 0)),
            pl.BlockSpec((tm, d_kv), lambda i: (i, 0)),
            pl.BlockSpec((1, KV_HEADS, V_ROWS, tm), lambda i: (i // tiles_per_seq, 0, 0, i % tiles_per_seq)),
        ],
        out_shape=[
            jax.ShapeDtypeStruct((rows, d_a), _BF16),
            jax.ShapeDtypeStruct((rows, d_b), _F32),
            jax.ShapeDtypeStruct((batch, KV_HEADS, n_qb, HEAD_DIM, qw), _BF16),
            jax.ShapeDtypeStruct((rows, d_kv), _BF16),
            jax.ShapeDtypeStruct((batch, KV_HEADS, V_ROWS, seq), _BF16),
        ],
        compiler_params=cparams,
        name="in_proj",
    )

    steps = n_qb // BLOCKS_PER_STEP
    last = steps - 1

    def next_batch(b, j):
        return jnp.minimum(b + (j == last).astype(jnp.int32), batch - 1)

    def next_block(b, j):
        return jnp.where(j == last, 0, BLOCKS_PER_STEP * (j + 1))

    attention = pl.pallas_call(
        _attention_kernel,
        grid=(batch, steps),
        in_specs=[
            pl.BlockSpec((1, KV_HEADS, BLOCKS_PER_STEP, HEAD_DIM, qw), lambda b, j: (b, 0, j, 0, 0)),
            pl.BlockSpec((1, KV_HEADS, 1, HEAD_DIM, qw), lambda b, j: (next_batch(b, j), 0, next_block(b, j), 0, 0)),
            pl.BlockSpec((seq, d_kv), lambda b, j: (b, 0)),
            pl.BlockSpec((seq, d_kv), lambda b, j: (next_batch(b, j), 0)),
            pl.BlockSpec((1, KV_HEADS, V_ROWS, seq), lambda b, j: (b, 0, 0, 0)),
            pl.BlockSpec((BLOCKS_PER_STEP * Q_BLOCK, d_b), lambda b, j: (b * steps + j, 0)),
        ],
        out_specs=pl.BlockSpec((BLOCKS_PER_STEP * Q_BLOCK, d_b), lambda b, j: (b * steps + j, 0)),
        out_shape=jax.ShapeDtypeStruct((rows, d_b), _BF16),
        scratch_shapes=[pltpu.VMEM((seq, qw), _F32)] * 4 + [pltpu.VMEM((8, qw), _F32)] * 4,
        compiler_params=pltpu.CompilerParams(
            dimension_semantics=("arbitrary", "arbitrary"), vmem_limit_bytes=VMEM_LIMIT_BYTES),
        name="attention",
    )

    out_proj = pl.pallas_call(
        _out_proj_kernel,
        grid=(rows // tm,),
        in_specs=[
            pl.BlockSpec((tm, d_model), lambda i: (i, 0)),
            pl.BlockSpec((tm, d_a), lambda i: (i, 0)),
            pl.BlockSpec((tm, d_b), lambda i: (i, 0)),
            _const_spec((d_mix, d_model)),
            _const_spec((1, d_model)),
        ],
        out_specs=pl.BlockSpec((tm, d_model), lambda i: (i, 0)),
        out_shape=jax.ShapeDtypeStruct((rows, d_model), _F32),
        compiler_params=cparams,
        name="out_proj",
    )
    return in_proj, attention, out_proj


def _in_proj_operands(dm, wl, pre_g, ln_g, ln_b, spatial_w, spatial_b, q_g, k_g):
    d_a, d_b, d_kv = dm.d_a, dm.d_b, dm.d_kv
    o_u, o_v, o_g, o_q = 0, d_a, 2 * d_a, 3 * d_a
    o_k, o_vv, o_gb = o_q + d_b, o_q + d_b + d_kv, o_q + d_b + 2 * d_kv
    w_nat = jnp.concatenate(
        [wl[:, o_u:o_u + d_a], wl[:, o_v:o_v + d_a], wl[:, o_g:o_g + d_a], wl[:, o_gb:o_gb + d_b]],
        axis=1).astype(_BF16)
    w_t = jnp.concatenate(
        [wl[:, o_q:o_q + d_b], wl[:, o_k:o_k + d_kv], wl[:, o_vv:o_vv + d_kv]], axis=1).T.astype(_BF16)
    ws = spatial_w.astype(_BF16)
    ws2 = jnp.concatenate([ws[0::2], ws[1::2]], axis=2)
    sbb = jnp.repeat(spatial_b.T, HEAD_DIM, axis=1)
    qg = jnp.broadcast_to(q_g[:, None], (HEAD_DIM, 128))
    kg = jnp.broadcast_to(k_g[:, None], (HEAD_DIM, 128))
    return pre_g[None], w_nat, w_t, ln_g[None], ln_b[None], ws2, sbb, qg, kg


def kernel(x, w_in, w_out, pre_g, post_g, ln_a_g, ln_a_b, spatial_w, spatial_b, q_norm_g, k_norm_g):
    dm = _Dims(x, w_in, ln_a_g)
    in_proj, attention, out_proj = _make_calls(dm)
    cos_t, sin_t = _rope_tables_t(dm.seq)
    x2 = x.reshape(dm.rows, dm.d_model)
    for l in range(w_in.shape[0]):
        ops = _in_proj_operands(dm, w_in[l], pre_g[l], ln_a_g[l], ln_a_b[l], spatial_w[l], spatial_b[l],
                                q_norm_g[l], k_norm_g[l])
        ya, sgb, qt, k, vt = in_proj(x2, *ops, cos_t, sin_t)
        yb = attention(qt, qt, k, k, vt, sgb)
        x2 = out_proj(x2, ya, yb, w_out[l].astype(_BF16), post_g[l][None])
    return x2.reshape(dm.batch, dm.seq, dm.d_model)
```

```python
import functools
import math

import jax
import jax.numpy as jnp
from jax import lax
from jax.experimental import pallas as pl
from jax.experimental.pallas import tpu as pltpu

GRID_W = 64
CHUNK = 128
Q_BLOCK = 128
H_A = 8
HEAD_DIM = 64
KV_HEADS = 2
ROPE_AXIS_DIM = HEAD_DIM // 2
ROPE_HALF = ROPE_AXIS_DIM // 2
ROPE_THETA = 10000.0
EPS = 1e-6

V_ROWS = HEAD_DIM + 16
ROW_TILE = 1024
SUB_ROWS = 512
KV_CHUNK = 256
SCORE_LAG = 2
BLOCKS_PER_STEP = 4
S_BUFFERS = 4
VMEM_LIMIT_BYTES = 56 * 1024 * 1024

_BF16 = jnp.bfloat16
_F32 = jnp.float32


def _silu(g):
    return g / (1.0 + jnp.exp(-g))


def _rope_t(x, cos, sin):
    n = ROPE_HALF
    a, b, c, d = x[:, 0:n], x[:, n:2 * n], x[:, 2 * n:3 * n], x[:, 3 * n:4 * n]
    cr, cc = cos[0:n][None], cos[n:2 * n][None]
    sr, sc = sin[0:n][None], sin[n:2 * n][None]
    return jnp.concatenate(
        [a * cr - b * sr, b * cr + a * sr, c * cc - d * sc, d * cc + c * sc], axis=1)


def _head_rmsnorm_t(x, g):
    r = lax.rsqrt(jnp.mean(x * x, axis=1, keepdims=True) + EPS)
    return x * r * g[None]


def _in_proj_rows(x, r0, preg_ref, wnat_ref, wt_ref, lng_ref, lnb_ref, ws2_ref, sbb_ref,
                  qg_ref, kg_ref, cos_ref, sin_ref, ya_ref, sgb_ref, qt_ref, k_ref, vt_ref, q_scale):
    nr = x.shape[0]
    rows = slice(r0, r0 + nr)
    d_a = lng_ref.shape[1]
    n_q_heads = qt_ref.shape[1] * (qt_ref.shape[4] // Q_BLOCK)
    q_per_kv = n_q_heads // KV_HEADS
    d_b = n_q_heads * HEAD_DIM
    d_kv = KV_HEADS * HEAD_DIM

    r = lax.rsqrt(jnp.mean(x * x, axis=-1, keepdims=True) + EPS)
    h = (x * r * preg_ref[...]).astype(_BF16)

    pn = jnp.dot(h, wnat_ref[...], preferred_element_type=_F32)
    u = pn[:, 0:d_a]
    v = pn[:, d_a:2 * d_a]
    ga = pn[:, 2 * d_a:3 * d_a]
    gb = pn[:, 3 * d_a:3 * d_a + d_b]
    sgb_ref[rows, :] = _silu(gb)

    mu = jnp.mean(v, axis=-1, keepdims=True)
    vc = v - mu
    var = jnp.mean(vc * vc, axis=-1, keepdims=True)
    vn = (vc * lax.rsqrt(var + EPS) * lng_ref[...] + lnb_ref[...]).astype(_BF16)

    lane = lax.broadcasted_iota(jnp.int32, (CHUNK, 2 * HEAD_DIM), 1)
    first = lane < HEAD_DIM
    zero = jnp.zeros((CHUNK, 2 * HEAD_DIM), _BF16)
    n_chunks = nr // CHUNK
    z_chunks = [[] for _ in range(n_chunks)]
    for c0 in range(0, n_chunks, 2):
        for p in range(d_a // (2 * HEAD_DIM)):
            rhs = []
            for c in (c0, c0 + 1):
                blk = vn[c * CHUNK:(c + 1) * CHUNK, p * 128:(p + 1) * 128]
                rhs.append(jnp.concatenate([jnp.where(first, blk, zero), jnp.where(first, zero, blk)], axis=0))
            zc = jnp.dot(ws2_ref[p], jnp.concatenate(rhs, axis=1), preferred_element_type=_F32)
            z_chunks[c0].append(zc[:, 0:128])
            z_chunks[c0 + 1].append(zc[:, 128:256])
    z = jnp.concatenate([jnp.concatenate(zb, axis=1) + sbb_ref[...] for zb in z_chunks], axis=0)
    ya_ref[rows, :] = (u * z * _silu(ga)).astype(_BF16)

    pt = lax.dot_general(wt_ref[...], h, (((1,), (1,)), ((), ())), preferred_element_type=_F32)
    cos = cos_ref[:, rows]
    sin = sin_ref[:, rows]
    reps = nr // qg_ref.shape[1]
    qg = jnp.concatenate([qg_ref[...]] * reps, axis=1)
    kg = jnp.concatenate([kg_ref[...]] * reps, axis=1)

    qt = pt[0:d_b].reshape(n_q_heads, HEAD_DIM, nr)
    qt = (_rope_t(_head_rmsnorm_t(qt, qg), cos, sin) * q_scale).astype(_BF16)
    for kvh in range(KV_HEADS):
        for jj in range(nr // Q_BLOCK):
            qt_ref[0, kvh, r0 // Q_BLOCK + jj] = jnp.concatenate(
                [qt[kvh * q_per_kv + g][:, jj * Q_BLOCK:(jj + 1) * Q_BLOCK] for g in range(q_per_kv)],
                axis=1)

    kt = pt[d_b:d_b + d_kv].reshape(KV_HEADS, HEAD_DIM, nr)
    kt = _rope_t(_head_rmsnorm_t(kt, kg), cos, sin).reshape(d_kv, nr)
    k_ref[rows, :] = kt.T.astype(_BF16)

    vt = pt[d_b + d_kv:d_b + 2 * d_kv].astype(_BF16)
    row = lax.broadcasted_iota(jnp.int32, (V_ROWS - HEAD_DIM, nr), 0)
    ones_rows = jnp.where(row == 0, 1.0, 0.0).astype(_BF16)
    for kvh in range(KV_HEADS):
        vt_ref[0, kvh, :, rows] = jnp.concatenate([vt[kvh * HEAD_DIM:(kvh + 1) * HEAD_DIM], ones_rows], axis=0)


def _residual_rows(x, ya, yb, wo_ref, postg_ref):
    y = jnp.concatenate([ya, yb], axis=1)
    out = jnp.dot(y, wo_ref[...], preferred_element_type=_F32)
    r = lax.rsqrt(jnp.mean(out * out, axis=-1, keepdims=True) + EPS)
    return x + out * r * postg_ref[...]


def _sub_tiles(tm):
    sub = min(SUB_ROWS, tm)
    return [slice(r0, r0 + sub) for r0 in range(0, tm, sub)]


def _in_proj_kernel(x_ref, *refs, q_scale):
    for rows in _sub_tiles(x_ref.shape[0]):
        _in_proj_rows(x_ref[rows, :], rows.start, *refs, q_scale)


def _boundary_kernel(x_ref, ya_ref, yb_ref, wo_ref, postg_ref, *refs, q_scale):
    xo_ref, in_refs = refs[-1], refs[:-1]
    subs = _sub_tiles(x_ref.shape[0])
    xs = []
    for rows in subs:
        xn = _residual_rows(x_ref[rows, :], ya_ref[rows, :], yb_ref[rows, :], wo_ref, postg_ref)
        xo_ref[rows, :] = xn
        xs.append(xn)
    for rows, xn in zip(subs, xs):
        _in_proj_rows(xn, rows.start, *in_refs, q_scale)


def _out_proj_kernel(x_ref, ya_ref, yb_ref, wo_ref, postg_ref, o_ref):
    for rows in _sub_tiles(x_ref.shape[0]):
        o_ref[rows, :] = _residual_rows(x_ref[rows, :], ya_ref[rows, :], yb_ref[rows, :], wo_ref, postg_ref)


def _score_weights(qt, kvh):
    zero = jnp.zeros_like(qt)
    return jnp.concatenate([zero] * kvh + [qt] + [zero] * (KV_HEADS - 1 - kvh), axis=0)


def _score_chunk(k_ref, w, s_ref, m8, c):
    rows = slice(c * KV_CHUNK, (c + 1) * KV_CHUNK)
    s = jnp.dot(k_ref[rows, :], w, preferred_element_type=_F32)
    s_ref[rows, :] = s
    cm = jnp.max(s.reshape(KV_CHUNK // 8, 8, s.shape[1]), axis=0)
    return cm if m8 is None else jnp.maximum(m8, cm)


def _value_chunk(s_ref, m, vt_ref, kvh, acc, c, row0):
    rows = slice(c * KV_CHUNK, (c + 1) * KV_CHUNK)
    srows = pl.ds(pl.multiple_of(row0 + c * KV_CHUNK, KV_CHUNK), KV_CHUNK)
    p = jnp.exp2(s_ref[srows, :] - m).astype(_BF16)
    part = jnp.dot(vt_ref[0, kvh, :, rows], p, preferred_element_type=_F32)
    return part if acc is None else acc + part


def _store_head_output(acc, sgb_ref, yb_ref, kvh, blk):
    d_pair = 2 * HEAD_DIM
    o = acc[0:HEAD_DIM] / acc[HEAD_DIM:HEAD_DIM + 1]
    n_pairs = o.shape[1] // (2 * Q_BLOCK)
    qrows = slice(blk * Q_BLOCK, (blk + 1) * Q_BLOCK)
    for pr in range(n_pairs):
        pair = jnp.concatenate(
            [o[:, (2 * pr) * Q_BLOCK:(2 * pr + 1) * Q_BLOCK],
             o[:, (2 * pr + 1) * Q_BLOCK:(2 * pr + 2) * Q_BLOCK]], axis=0)
        lo = (kvh * n_pairs + pr) * d_pair
        yb_ref[qrows, lo:lo + d_pair] = (pair.T * sgb_ref[qrows, lo:lo + d_pair]).astype(_BF16)


def _scores_only(qt, kvh, k_ref, s_ref, m_ref):
    w = _score_weights(qt, kvh)
    m8 = None
    for c in range(k_ref.shape[0] // KV_CHUNK):
        m8 = _score_chunk(k_ref, w, s_ref, m8, c)
    m_ref[...] = m8


def _values_and_scores(s_r_ref, m_r_ref, kvh, blk, qt, k_ref, s_w_ref, m_w_ref, vt_ref, sgb_ref, yb_ref, row0):
    n = k_ref.shape[0] // KV_CHUNK
    w = _score_weights(qt, kvh)
    m = jnp.max(m_r_ref[...], axis=0, keepdims=True)
    m8 = None
    acc = None
    for c in range(n):
        acc = _value_chunk(s_r_ref, m, vt_ref, kvh, acc, c, row0)
        if c >= SCORE_LAG:
            m8 = _score_chunk(k_ref, w, s_w_ref, m8, c - SCORE_LAG)
    _store_head_output(acc, sgb_ref, yb_ref, kvh, blk)
    for c in range(n - SCORE_LAG, n):
        m8 = _score_chunk(k_ref, w, s_w_ref, m8, c)
    m_w_ref[...] = m8


def _attention_kernel(qt_ref, qt_next_ref, k_ref, k_next_ref, vt_ref, sgb_ref, yb_ref, *scratch):
    s_refs, m_refs = scratch[:S_BUFFERS], scratch[S_BUFFERS:]
    n_units = BLOCKS_PER_STEP * KV_HEADS

    @pl.when((pl.program_id(0) == 0) & (pl.program_id(1) == 0))
    def _():
        for kvh in range(KV_HEADS):
            _scores_only(qt_ref[0, kvh, 0], kvh, k_ref, s_refs[kvh], m_refs[kvh])

    row0 = jnp.minimum(pl.program_id(1), 0)
    for u in range(n_units):
        blk, kvh = divmod(u, KV_HEADS)
        if blk + 1 < BLOCKS_PER_STEP:
            qt_ahead, k_ahead = qt_ref[0, kvh, blk + 1], k_ref
        else:
            qt_ahead, k_ahead = qt_next_ref[0, kvh, 0], k_next_ref
        _values_and_scores(
            s_refs[u % S_BUFFERS], m_refs[u % S_BUFFERS], kvh, blk, qt_ahead, k_ahead,
            s_refs[(u + KV_HEADS) % S_BUFFERS], m_refs[(u + KV_HEADS) % S_BUFFERS],
            vt_ref, sgb_ref, yb_ref, row0)


def _rope_tables_t(seq_len):
    pos = jnp.arange(seq_len, dtype=jnp.int32)
    row = (pos // GRID_W).astype(_F32)
    col = (pos % GRID_W).astype(_F32)
    inv = ROPE_THETA ** (-jnp.arange(0, ROPE_AXIS_DIM, 2, dtype=_F32) / ROPE_AXIS_DIM)
    ang = jnp.concatenate([inv[:, None] * row[None, :], inv[:, None] * col[None, :]], axis=0)
    return jnp.cos(ang), jnp.sin(ang)


def _const_spec(shape):
    return pl.BlockSpec(shape, lambda *_: (0,) * len(shape), pipeline_mode=pl.Buffered(1))


class _Dims:
    def __init__(self, x, w_in, ln_a_g):
        self.batch, self.seq, self.d_model = x.shape
        self.d_a = ln_a_g.shape[1]
        self.d_kv = KV_HEADS * HEAD_DIM
        self.d_b = (w_in.shape[2] - 3 * self.d_a - 2 * self.d_kv) // 2
        self.n_q_heads = self.d_b // HEAD_DIM
        self.q_per_kv = self.n_q_heads // KV_HEADS
        self.rows = self.batch * self.seq
        self.tm = min(ROW_TILE, self.seq)
        self.n_qb = self.seq // Q_BLOCK
        self.qw = self.q_per_kv * Q_BLOCK
        assert self.seq % self.tm == 0 and self.tm % min(SUB_ROWS, self.tm) == 0
        assert min(SUB_ROWS, self.tm) % (2 * CHUNK) == 0 and self.seq % GRID_W == 0
        assert self.d_a == H_A * HEAD_DIM and self.q_per_kv % 2 == 0
        assert self.n_qb % BLOCKS_PER_STEP == 0 and S_BUFFERS == 2 * KV_HEADS
        assert (BLOCKS_PER_STEP * KV_HEADS) % S_BUFFERS == 0


def _make_calls(dm):
    batch, seq, d_model, d_a, d_b, d_kv = dm.batch, dm.seq, dm.d_model, dm.d_a, dm.d_b, dm.d_kv
    rows, tm, n_qb, qw = dm.rows, dm.tm, dm.n_qb, dm.qw
    d_mix = d_a + d_b
    tiles_per_seq = seq // tm
    qb_per_tile = tm // Q_BLOCK
    q_scale = HEAD_DIM ** -0.5 * math.log2(math.e)
    cparams = pltpu.CompilerParams(dimension_semantics=("arbitrary",), vmem_limit_bytes=VMEM_LIMIT_BYTES)

    def row_spec(width):
        return pl.BlockSpec((tm, width), lambda i: (i, 0))

    in_proj_in_specs = [
        _const_spec((1, d_model)),
        _const_spec((d_model, 3 * d_a + d_b)),
        _const_spec((d_b + 2 * d_kv, d_model)),
        _const_spec((1, d_a)),
        _const_spec((1, d_a)),
        _const_spec((H_A // 2, CHUNK, 2 * CHUNK)),
        _const_spec((CHUNK, d_a)),
        _const_spec((HEAD_DIM, 128)),
        _const_spec((HEAD_DIM, 128)),
        pl.BlockSpec((ROPE_AXIS_DIM, tm), lambda i: (0, i % tiles_per_seq)),
        pl.BlockSpec((ROPE_AXIS_DIM, tm), lambda i: (0, i % tiles_per_seq)),
    ]
    in_proj_out_specs = [
        row_spec(d_a),
        row_spec(d_b),
        pl.BlockSpec((1, KV_HEADS, qb_per_tile, HEAD_DIM, qw),
                     lambda i: (i // tiles_per_seq, 0, i % tiles_per_seq, 0, 0)),
        row_spec(d_kv),
        pl.BlockSpec((1, KV_HEADS, V_ROWS, tm), lambda i: (i // tiles_per_seq, 0, 0, i % tiles_per_seq)),
    ]
    in_proj_out_shape = [
        jax.ShapeDtypeStruct((rows, d_a), _BF16),
        jax.ShapeDtypeStruct((rows, d_b), _F32),
        jax.ShapeDtypeStruct((batch, KV_HEADS, n_qb, HEAD_DIM, qw), _BF16),
        jax.ShapeDtypeStruct((rows, d_kv), _BF16),
        jax.ShapeDtypeStruct((batch, KV_HEADS, V_ROWS, seq), _BF16),
    ]
    out_proj_in_specs = [
        row_spec(d_model), row_spec(d_a), row_spec(d_b), _const_spec((d_mix, d_model)), _const_spec((1, d_model))]
    x_shape = jax.ShapeDtypeStruct((rows, d_model), _F32)

    in_proj = pl.pallas_call(
        functools.partial(_in_proj_kernel, q_scale=q_scale),
        grid=(rows // tm,),
        in_specs=[row_spec(d_model)] + in_proj_in_specs,
        out_specs=in_proj_out_specs,
        out_shape=in_proj_out_shape,
        compiler_params=cparams,
        name="in_proj",
    )

    boundary = pl.pallas_call(
        functools.partial(_boundary_kernel, q_scale=q_scale),
        grid=(rows // tm,),
        in_specs=out_proj_in_specs + in_proj_in_specs,
        out_specs=in_proj_out_specs + [row_spec(d_model)],
        out_shape=in_proj_out_shape + [x_shape],
        compiler_params=cparams,
        name="boundary",
    )

    out_proj = pl.pallas_call(
        _out_proj_kernel,
        grid=(rows // tm,),
        in_specs=out_proj_in_specs,
        out_specs=row_spec(d_model),
        out_shape=x_shape,
        compiler_params=cparams,
        name="out_proj",
    )

    steps = n_qb // BLOCKS_PER_STEP
    last = steps - 1

    def next_batch(b, j):
        return jnp.minimum(b + (j == last).astype(jnp.int32), batch - 1)

    def next_block(b, j):
        return jnp.where(j == last, 0, BLOCKS_PER_STEP * (j + 1))

    attention = pl.pallas_call(
        _attention_kernel,
        grid=(batch, steps),
        in_specs=[
            pl.BlockSpec((1, KV_HEADS, BLOCKS_PER_STEP, HEAD_DIM, qw), lambda b, j: (b, 0, j, 0, 0)),
            pl.BlockSpec((1, KV_HEADS, 1, HEAD_DIM, qw), lambda b, j: (next_batch(b, j), 0, next_block(b, j), 0, 0)),
            pl.BlockSpec((seq, d_kv), lambda b, j: (b, 0)),
            pl.BlockSpec((seq, d_kv), lambda b, j: (next_batch(b, j), 0)),
            pl.BlockSpec((1, KV_HEADS, V_ROWS, seq), lambda b, j: (b, 0, 0, 0)),
            pl.BlockSpec((BLOCKS_PER_STEP * Q_BLOCK, d_b), lambda b, j: (b * steps + j, 0)),
        ],
        out_specs=pl.BlockSpec((BLOCKS_PER_STEP * Q_BLOCK, d_b), lambda b, j: (b * steps + j, 0)),
        out_shape=jax.ShapeDtypeStruct((rows, d_b), _BF16),
        scratch_shapes=[pltpu.VMEM((seq, qw), _F32)] * S_BUFFERS + [pltpu.VMEM((8, qw), _F32)] * S_BUFFERS,
        compiler_params=pltpu.CompilerParams(
            dimension_semantics=("arbitrary", "arbitrary"), vmem_limit_bytes=VMEM_LIMIT_BYTES),
        name="attention",
    )
    return in_proj, boundary, attention, out_proj


def _in_proj_operands(dm, wl, pre_g, ln_g, ln_b, spatial_w, spatial_b, q_g, k_g):
    d_a, d_b, d_kv = dm.d_a, dm.d_b, dm.d_kv
    o_u, o_v, o_g, o_q = 0, d_a, 2 * d_a, 3 * d_a
    o_k, o_vv, o_gb = o_q + d_b, o_q + d_b + d_kv, o_q + d_b + 2 * d_kv
    w_nat = jnp.concatenate(
        [wl[:, o_u:o_u + d_a], wl[:, o_v:o_v + d_a], wl[:, o_g:o_g + d_a], wl[:, o_gb:o_gb + d_b]],
        axis=1).astype(_BF16)
    w_t = jnp.concatenate(
        [wl[:, o_q:o_q + d_b], wl[:, o_k:o_k + d_kv], wl[:, o_vv:o_vv + d_kv]], axis=1).T.astype(_BF16)
    ws = spatial_w.astype(_BF16)
    ws2 = jnp.concatenate([ws[0::2], ws[1::2]], axis=2)
    sbb = jnp.repeat(spatial_b.T, HEAD_DIM, axis=1)
    qg = jnp.broadcast_to(q_g[:, None], (HEAD_DIM, 128))
    kg = jnp.broadcast_to(k_g[:, None], (HEAD_DIM, 128))
    return pre_g[None], w_nat, w_t, ln_g[None], ln_b[None], ws2, sbb, qg, kg


def kernel(x, w_in, w_out, pre_g, post_g, ln_a_g, ln_a_b, spatial_w, spatial_b, q_norm_g, k_norm_g):
    dm = _Dims(x, w_in, ln_a_g)
    in_proj, boundary, attention, out_proj = _make_calls(dm)
    cos_t, sin_t = _rope_tables_t(dm.seq)
    x2 = x.reshape(dm.rows, dm.d_model)
    depth = w_in.shape[0]
    ya = yb = None
    for l in range(depth):
        ops = _in_proj_operands(dm, w_in[l], pre_g[l], ln_a_g[l], ln_a_b[l], spatial_w[l], spatial_b[l],
                                q_norm_g[l], k_norm_g[l])
        if l == 0:
            ya, sgb, qt, k, vt = in_proj(x2, *ops, cos_t, sin_t)
        else:
            ya, sgb, qt, k, vt, x2 = boundary(
                x2, ya, yb, w_out[l - 1].astype(_BF16), post_g[l - 1][None], *ops, cos_t, sin_t)
        yb = attention(qt, qt, k, k, vt, sgb)
    x2 = out_proj(x2, ya, yb, w_out[depth - 1].astype(_BF16), post_g[depth - 1][None])
    return x2.reshape(dm.batch, dm.seq, dm.d_model)
```

```python
import functools
import math

import jax
import jax.numpy as jnp
from jax import lax
from jax.experimental import pallas as pl
from jax.experimental.pallas import tpu as pltpu

GRID_W = 64
CHUNK = 128
Q_BLOCK = 128
H_A = 8
HEAD_DIM = 64
KV_HEADS = 2
ROPE_AXIS_DIM = HEAD_DIM // 2
ROPE_HALF = ROPE_AXIS_DIM // 2
ROPE_THETA = 10000.0
EPS = 1e-6

V_ROWS = HEAD_DIM + 16
ROW_TILE = 1024
SUB_ROWS = 512
KV_CHUNK = 256
SCORE_LAG = 2
BLOCKS_PER_STEP = 4
S_BUFFERS = 4
VMEM_LIMIT_BYTES = 56 * 1024 * 1024

_BF16 = jnp.bfloat16
_F32 = jnp.float32


def _silu(g):
    h = 0.5 * g
    return h + h * jnp.tanh(h)


def _rope_t(x, cos, sin):
    n = ROPE_HALF
    a, b, c, d = x[:, 0:n], x[:, n:2 * n], x[:, 2 * n:3 * n], x[:, 3 * n:4 * n]
    cr, cc = cos[0:n][None], cos[n:2 * n][None]
    sr, sc = sin[0:n][None], sin[n:2 * n][None]
    return jnp.concatenate(
        [a * cr - b * sr, b * cr + a * sr, c * cc - d * sc, d * cc + c * sc], axis=1)


def _head_rmsnorm_t(x, g):
    r = lax.rsqrt(jnp.mean(x * x, axis=1, keepdims=True) + EPS)
    return x * r * g[None]


def _in_proj_rows(x, r0, preg_ref, wnat_ref, wt_ref, lng_ref, lnb_ref, ws2_ref, sbb_ref,
                  qg_ref, kg_ref, cos_ref, sin_ref, ya_ref, sgb_ref, qt_ref, k_ref, vt_ref, q_scale):
    nr = x.shape[0]
    rows = slice(r0, r0 + nr)
    d_a = lng_ref.shape[1]
    n_q_heads = qt_ref.shape[1] * (qt_ref.shape[4] // Q_BLOCK)
    q_per_kv = n_q_heads // KV_HEADS
    d_b = n_q_heads * HEAD_DIM
    d_kv = KV_HEADS * HEAD_DIM

    r = lax.rsqrt(jnp.mean(x * x, axis=-1, keepdims=True) + EPS)
    h = (x * r * preg_ref[...]).astype(_BF16)

    pn = jnp.dot(h, wnat_ref[...], preferred_element_type=_F32)
    u = pn[:, 0:d_a]
    v = pn[:, d_a:2 * d_a]
    ga = pn[:, 2 * d_a:3 * d_a]
    gb = pn[:, 3 * d_a:3 * d_a + d_b]
    sgb_ref[rows, :] = _silu(gb)

    mu = jnp.mean(v, axis=-1, keepdims=True)
    vc = v - mu
    var = jnp.mean(vc * vc, axis=-1, keepdims=True)
    vn = (vc * lax.rsqrt(var + EPS) * lng_ref[...] + lnb_ref[...]).astype(_BF16)

    lane = lax.broadcasted_iota(jnp.int32, (CHUNK, 2 * HEAD_DIM), 1)
    first = lane < HEAD_DIM
    zero = jnp.zeros((CHUNK, 2 * HEAD_DIM), _BF16)
    n_chunks = nr // CHUNK
    z_chunks = [[] for _ in range(n_chunks)]
    for c0 in range(0, n_chunks, 2):
        for p in range(d_a // (2 * HEAD_DIM)):
            rhs = []
            for c in (c0, c0 + 1):
                blk = vn[c * CHUNK:(c + 1) * CHUNK, p * 128:(p + 1) * 128]
                rhs.append(jnp.concatenate([jnp.where(first, blk, zero), jnp.where(first, zero, blk)], axis=0))
            zc = jnp.dot(ws2_ref[p], jnp.concatenate(rhs, axis=1), preferred_element_type=_F32)
            z_chunks[c0].append(zc[:, 0:128])
            z_chunks[c0 + 1].append(zc[:, 128:256])
    z = jnp.concatenate([jnp.concatenate(zb, axis=1) + sbb_ref[...] for zb in z_chunks], axis=0)
    ya_ref[rows, :] = (u * z * _silu(ga)).astype(_BF16)

    pt = lax.dot_general(wt_ref[...], h, (((1,), (1,)), ((), ())), preferred_element_type=_F32)
    cos = cos_ref[:, rows]
    sin = sin_ref[:, rows]
    reps = nr // qg_ref.shape[1]
    qg = jnp.concatenate([qg_ref[...]] * reps, axis=1)
    kg = jnp.concatenate([kg_ref[...]] * reps, axis=1)

    qt = pt[0:d_b].reshape(n_q_heads, HEAD_DIM, nr)
    qt = (_rope_t(_head_rmsnorm_t(qt, qg), cos, sin) * q_scale).astype(_BF16)
    for kvh in range(KV_HEADS):
        for jj in range(nr // Q_BLOCK):
            qt_ref[0, kvh, r0 // Q_BLOCK + jj] = jnp.concatenate(
                [qt[kvh * q_per_kv + g][:, jj * Q_BLOCK:(jj + 1) * Q_BLOCK] for g in range(q_per_kv)],
                axis=1)

    kt = pt[d_b:d_b + d_kv].reshape(KV_HEADS, HEAD_DIM, nr)
    kt = _rope_t(_head_rmsnorm_t(kt, kg), cos, sin).reshape(d_kv, nr)
    k_ref[rows, :] = kt.T.astype(_BF16)

    vt = pt[d_b + d_kv:d_b + 2 * d_kv].astype(_BF16)
    row = lax.broadcasted_iota(jnp.int32, (V_ROWS - HEAD_DIM, nr), 0)
    ones_rows = jnp.where(row == 0, 1.0, 0.0).astype(_BF16)
    for kvh in range(KV_HEADS):
        vt_ref[0, kvh, :, rows] = jnp.concatenate([vt[kvh * HEAD_DIM:(kvh + 1) * HEAD_DIM], ones_rows], axis=0)


def _residual_rows(x, ya, yb, wo_ref, postg_ref):
    y = jnp.concatenate([ya, yb], axis=1)
    out = jnp.dot(y, wo_ref[...], preferred_element_type=_F32)
    r = lax.rsqrt(jnp.mean(out * out, axis=-1, keepdims=True) + EPS)
    return x + out * r * postg_ref[...]


def _sub_tiles(tm):
    sub = min(SUB_ROWS, tm)
    return [slice(r0, r0 + sub) for r0 in range(0, tm, sub)]


def _in_proj_kernel(x_ref, *refs, q_scale):
    for rows in _sub_tiles(x_ref.shape[0]):
        _in_proj_rows(x_ref[rows, :], rows.start, *refs, q_scale)


def _boundary_kernel(x_ref, ya_ref, yb_ref, wo_ref, postg_ref, *refs, q_scale):
    xo_ref, in_refs = refs[-1], refs[:-1]
    subs = _sub_tiles(x_ref.shape[0])
    xs = []
    for rows in subs:
        xn = _residual_rows(x_ref[rows, :], ya_ref[rows, :], yb_ref[rows, :], wo_ref, postg_ref)
        xo_ref[rows, :] = xn
        xs.append(xn)
    for rows, xn in zip(subs, xs):
        _in_proj_rows(xn, rows.start, *in_refs, q_scale)


def _out_proj_kernel(x_ref, ya_ref, yb_ref, wo_ref, postg_ref, o_ref):
    for rows in _sub_tiles(x_ref.shape[0]):
        o_ref[rows, :] = _residual_rows(x_ref[rows, :], ya_ref[rows, :], yb_ref[rows, :], wo_ref, postg_ref)


def _score_weights(qt, kvh):
    zero = jnp.zeros_like(qt)
    return jnp.concatenate([zero] * kvh + [qt] + [zero] * (KV_HEADS - 1 - kvh), axis=0)


def _score_chunk(k_ref, w, s_ref, m8, c):
    rows = slice(c * KV_CHUNK, (c + 1) * KV_CHUNK)
    s = jnp.dot(k_ref[rows, :], w, preferred_element_type=_F32)
    s_ref[rows, :] = s
    cm = jnp.max(s.reshape(KV_CHUNK // 8, 8, s.shape[1]), axis=0)
    return cm if m8 is None else jnp.maximum(m8, cm)


def _value_chunk(s_ref, m, vt_ref, kvh, acc, c, row0):
    rows = slice(c * KV_CHUNK, (c + 1) * KV_CHUNK)
    srows = pl.ds(pl.multiple_of(row0 + c * KV_CHUNK, KV_CHUNK), KV_CHUNK)
    p = jnp.exp2(s_ref[srows, :] - m).astype(_BF16)
    part = jnp.dot(vt_ref[0, kvh, :, rows], p, preferred_element_type=_F32)
    return part if acc is None else acc + part


def _store_head_output(acc, sgb_ref, yb_ref, kvh, blk):
    d_pair = 2 * HEAD_DIM
    o = acc[0:HEAD_DIM] / acc[HEAD_DIM:HEAD_DIM + 1]
    n_pairs = o.shape[1] // (2 * Q_BLOCK)
    qrows = slice(blk * Q_BLOCK, (blk + 1) * Q_BLOCK)
    for pr in range(n_pairs):
        pair = jnp.concatenate(
            [o[:, (2 * pr) * Q_BLOCK:(2 * pr + 1) * Q_BLOCK],
             o[:, (2 * pr + 1) * Q_BLOCK:(2 * pr + 2) * Q_BLOCK]], axis=0)
        lo = (kvh * n_pairs + pr) * d_pair
        yb_ref[qrows, lo:lo + d_pair] = (pair.T * sgb_ref[qrows, lo:lo + d_pair]).astype(_BF16)


def _scores_only(qt, kvh, k_ref, s_ref, m_ref):
    w = _score_weights(qt, kvh)
    m8 = None
    for c in range(k_ref.shape[0] // KV_CHUNK):
        m8 = _score_chunk(k_ref, w, s_ref, m8, c)
    m_ref[...] = m8


def _values_and_scores(s_r_ref, m_r_ref, kvh, blk, qt, k_ref, s_w_ref, m_w_ref, vt_ref, sgb_ref, yb_ref, row0):
    n = k_ref.shape[0] // KV_CHUNK
    w = _score_weights(qt, kvh)
    m = jnp.max(m_r_ref[...], axis=0, keepdims=True)
    m8 = None
    acc = None
    for c in range(n):
        acc = _value_chunk(s_r_ref, m, vt_ref, kvh, acc, c, row0)
        if c >= SCORE_LAG:
            m8 = _score_chunk(k_ref, w, s_w_ref, m8, c - SCORE_LAG)
    _store_head_output(acc, sgb_ref, yb_ref, kvh, blk)
    for c in range(n - SCORE_LAG, n):
        m8 = _score_chunk(k_ref, w, s_w_ref, m8, c)
    m_w_ref[...] = m8


def _attention_kernel(qt_ref, qt_next_ref, k_ref, k_next_ref, vt_ref, sgb_ref, yb_ref, *scratch):
    s_refs, m_refs = scratch[:S_BUFFERS], scratch[S_BUFFERS:]
    n_units = BLOCKS_PER_STEP * KV_HEADS

    @pl.when((pl.program_id(0) == 0) & (pl.program_id(1) == 0))
    def _():
        for kvh in range(KV_HEADS):
            _scores_only(qt_ref[0, kvh, 0], kvh, k_ref, s_refs[kvh], m_refs[kvh])

    row0 = jnp.minimum(pl.program_id(1), 0)
    for u in range(n_units):
        blk, kvh = divmod(u, KV_HEADS)
        if blk + 1 < BLOCKS_PER_STEP:
            qt_ahead, k_ahead = qt_ref[0, kvh, blk + 1], k_ref
        else:
            qt_ahead, k_ahead = qt_next_ref[0, kvh, 0], k_next_ref
        _values_and_scores(
            s_refs[u % S_BUFFERS], m_refs[u % S_BUFFERS], kvh, blk, qt_ahead, k_ahead,
            s_refs[(u + KV_HEADS) % S_BUFFERS], m_refs[(u + KV_HEADS) % S_BUFFERS],
            vt_ref, sgb_ref, yb_ref, row0)


def _rope_tables_t(seq_len):
    pos = jnp.arange(seq_len, dtype=jnp.int32)
    row = (pos // GRID_W).astype(_F32)
    col = (pos % GRID_W).astype(_F32)
    inv = ROPE_THETA ** (-jnp.arange(0, ROPE_AXIS_DIM, 2, dtype=_F32) / ROPE_AXIS_DIM)
    ang = jnp.concatenate([inv[:, None] * row[None, :], inv[:, None] * col[None, :]], axis=0)
    return jnp.cos(ang), jnp.sin(ang)


def _const_spec(shape):
    return pl.BlockSpec(shape, lambda *_: (0,) * len(shape), pipeline_mode=pl.Buffered(1))


class _Dims:
    def __init__(self, x, w_in, ln_a_g):
        self.batch, self.seq, self.d_model = x.shape
        self.d_a = ln_a_g.shape[1]
        self.d_kv = KV_HEADS * HEAD_DIM
        self.d_b = (w_in.shape[2] - 3 * self.d_a - 2 * self.d_kv) // 2
        self.n_q_heads = self.d_b // HEAD_DIM
        self.q_per_kv = self.n_q_heads // KV_HEADS
        self.rows = self.batch * self.seq
        self.tm = min(ROW_TILE, self.seq)
        self.n_qb = self.seq // Q_BLOCK
        self.qw = self.q_per_kv * Q_BLOCK
        assert self.seq % self.tm == 0 and self.tm % min(SUB_ROWS, self.tm) == 0
        assert min(SUB_ROWS, self.tm) % (2 * CHUNK) == 0 and self.seq % GRID_W == 0
        assert self.d_a == H_A * HEAD_DIM and self.q_per_kv % 2 == 0
        assert self.n_qb % BLOCKS_PER_STEP == 0 and S_BUFFERS == 2 * KV_HEADS
        assert (BLOCKS_PER_STEP * KV_HEADS) % S_BUFFERS == 0


def _make_calls(dm):
    batch, seq, d_model, d_a, d_b, d_kv = dm.batch, dm.seq, dm.d_model, dm.d_a, dm.d_b, dm.d_kv
    rows, tm, n_qb, qw = dm.rows, dm.tm, dm.n_qb, dm.qw
    d_mix = d_a + d_b
    tiles_per_seq = seq // tm
    qb_per_tile = tm // Q_BLOCK
    q_scale = HEAD_DIM ** -0.5 * math.log2(math.e)
    cparams = pltpu.CompilerParams(dimension_semantics=("arbitrary",), vmem_limit_bytes=VMEM_LIMIT_BYTES)

    def row_spec(width):
        return pl.BlockSpec((tm, width), lambda i: (i, 0))

    in_proj_in_specs = [
        _const_spec((1, d_model)),
        _const_spec((d_model, 3 * d_a + d_b)),
        _const_spec((d_b + 2 * d_kv, d_model)),
        _const_spec((1, d_a)),
        _const_spec((1, d_a)),
        _const_spec((H_A // 2, CHUNK, 2 * CHUNK)),
        _const_spec((CHUNK, d_a)),
        _const_spec((HEAD_DIM, 128)),
        _const_spec((HEAD_DIM, 128)),
        pl.BlockSpec((ROPE_AXIS_DIM, tm), lambda i: (0, i % tiles_per_seq)),
        pl.BlockSpec((ROPE_AXIS_DIM, tm), lambda i: (0, i % tiles_per_seq)),
    ]
    in_proj_out_specs = [
        row_spec(d_a),
        row_spec(d_b),
        pl.BlockSpec((1, KV_HEADS, qb_per_tile, HEAD_DIM, qw),
                     lambda i: (i // tiles_per_seq, 0, i % tiles_per_seq, 0, 0)),
        row_spec(d_kv),
        pl.BlockSpec((1, KV_HEADS, V_ROWS, tm), lambda i: (i // tiles_per_seq, 0, 0, i % tiles_per_seq)),
    ]
    in_proj_out_shape = [
        jax.ShapeDtypeStruct((rows, d_a), _BF16),
        jax.ShapeDtypeStruct((rows, d_b), _F32),
        jax.ShapeDtypeStruct((batch, KV_HEADS, n_qb, HEAD_DIM, qw), _BF16),
        jax.ShapeDtypeStruct((rows, d_kv), _BF16),
        jax.ShapeDtypeStruct((batch, KV_HEADS, V_ROWS, seq), _BF16),
    ]
    out_proj_in_specs = [
        row_spec(d_model), row_spec(d_a), row_spec(d_b), _const_spec((d_mix, d_model)), _const_spec((1, d_model))]
    x_shape = jax.ShapeDtypeStruct((rows, d_model), _F32)

    in_proj = pl.pallas_call(
        functools.partial(_in_proj_kernel, q_scale=q_scale),
        grid=(rows // tm,),
        in_specs=[row_spec(d_model)] + in_proj_in_specs,
        out_specs=in_proj_out_specs,
        out_shape=in_proj_out_shape,
        compiler_params=cparams,
        name="in_proj",
    )

    boundary = pl.pallas_call(
        functools.partial(_boundary_kernel, q_scale=q_scale),
        grid=(rows // tm,),
        in_specs=out_proj_in_specs + in_proj_in_specs,
        out_specs=in_proj_out_specs + [row_spec(d_model)],
        out_shape=in_proj_out_shape + [x_shape],
        compiler_params=cparams,
        name="boundary",
    )

    out_proj = pl.pallas_call(
        _out_proj_kernel,
        grid=(rows // tm,),
        in_specs=out_proj_in_specs,
        out_specs=row_spec(d_model),
        out_shape=x_shape,
        compiler_params=cparams,
        name="out_proj",
    )

    steps = n_qb // BLOCKS_PER_STEP
    last = steps - 1

    def next_batch(b, j):
        return jnp.minimum(b + (j == last).astype(jnp.int32), batch - 1)

    def next_block(b, j):
        return jnp.where(j == last, 0, BLOCKS_PER_STEP * (j + 1))

    attention = pl.pallas_call(
        _attention_kernel,
        grid=(batch, steps),
        in_specs=[
            pl.BlockSpec((1, KV_HEADS, BLOCKS_PER_STEP, HEAD_DIM, qw), lambda b, j: (b, 0, j, 0, 0)),
            pl.BlockSpec((1, KV_HEADS, 1, HEAD_DIM, qw), lambda b, j: (next_batch(b, j), 0, next_block(b, j), 0, 0)),
            pl.BlockSpec((seq, d_kv), lambda b, j: (b, 0)),
            pl.BlockSpec((seq, d_kv), lambda b, j: (next_batch(b, j), 0)),
            pl.BlockSpec((1, KV_HEADS, V_ROWS, seq), lambda b, j: (b, 0, 0, 0)),
            pl.BlockSpec((BLOCKS_PER_STEP * Q_BLOCK, d_b), lambda b, j: (b * steps + j, 0)),
        ],
        out_specs=pl.BlockSpec((BLOCKS_PER_STEP * Q_BLOCK, d_b), lambda b, j: (b * steps + j, 0)),
        out_shape=jax.ShapeDtypeStruct((rows, d_b), _BF16),
        scratch_shapes=[pltpu.VMEM((seq, qw), _F32)] * S_BUFFERS + [pltpu.VMEM((8, qw), _F32)] * S_BUFFERS,
        compiler_params=pltpu.CompilerParams(
            dimension_semantics=("arbitrary", "arbitrary"), vmem_limit_bytes=VMEM_LIMIT_BYTES),
        name="attention",
    )
    return in_proj, boundary, attention, out_proj


def _in_proj_operands(dm, w_in, pre_g, ln_g, ln_b, spatial_w, spatial_b, q_g, k_g):
    d_a, d_b, d_kv = dm.d_a, dm.d_b, dm.d_kv
    depth = w_in.shape[0]
    o_q = 3 * d_a
    o_gb = o_q + d_b + 2 * d_kv
    w_nat = jnp.concatenate([w_in[:, :, 0:o_q], w_in[:, :, o_gb:o_gb + d_b]], axis=2).astype(_BF16)
    w_t = jnp.swapaxes(w_in[:, :, o_q:o_gb], 1, 2).astype(_BF16)
    ws = spatial_w.astype(_BF16)
    ws2 = jnp.concatenate([ws[:, 0::2], ws[:, 1::2]], axis=3)
    sbb = jnp.repeat(jnp.swapaxes(spatial_b, 1, 2), HEAD_DIM, axis=2)
    qg = jnp.broadcast_to(q_g[:, :, None], (depth, HEAD_DIM, 128))
    kg = jnp.broadcast_to(k_g[:, :, None], (depth, HEAD_DIM, 128))
    return pre_g[:, None], w_nat, w_t, ln_g[:, None], ln_b[:, None], ws2, sbb, qg, kg


def kernel(x, w_in, w_out, pre_g, post_g, ln_a_g, ln_a_b, spatial_w, spatial_b, q_norm_g, k_norm_g):
    dm = _Dims(x, w_in, ln_a_g)
    in_proj, boundary, attention, out_proj = _make_calls(dm)
    cos_t, sin_t = _rope_tables_t(dm.seq)
    x2 = x.reshape(dm.rows, dm.d_model)
    depth = w_in.shape[0]
    all_ops = _in_proj_operands(dm, w_in, pre_g, ln_a_g, ln_a_b, spatial_w, spatial_b, q_norm_g, k_norm_g)
    w_out = w_out.astype(_BF16)
    post_g = post_g[:, None]
    ya = yb = None
    for l in range(depth):
        ops = [a[l] for a in all_ops]
        if l == 0:
            ya, sgb, qt, k, vt = in_proj(x2, *ops, cos_t, sin_t)
        else:
            ya, sgb, qt, k, vt, x2 = boundary(x2, ya, yb, w_out[l - 1], post_g[l - 1], *ops, cos_t, sin_t)
        yb = attention(qt, qt, k, k, vt, sgb)
    x2 = out_proj(x2, ya, yb, w_out[depth - 1], post_g[depth - 1])
    return x2.reshape(dm.batch, dm.seq, dm.d_model)
```

```python
import functools
import math

import jax
import jax.numpy as jnp
from jax import lax
from jax.experimental import pallas as pl
from jax.experimental.pallas import tpu as pltpu

GRID_W = 64
CHUNK = 128
Q_BLOCK = 128
H_A = 8
HEAD_DIM = 64
KV_HEADS = 2
ROPE_AXIS_DIM = HEAD_DIM // 2
ROPE_HALF = ROPE_AXIS_DIM // 2
ROPE_THETA = 10000.0
EPS = 1e-6

V_ROWS = HEAD_DIM + 16
ROW_TILE = 1024
SUB_ROWS = 512
KV_CHUNK = 256
BLOCKS_PER_STEP = 4
S_BUFFERS = 4
VMEM_LIMIT_BYTES = 56 * 1024 * 1024

_BF16 = jnp.bfloat16
_F32 = jnp.float32


def _silu(g):
    h = 0.5 * g
    return h + h * jnp.tanh(h)


def _rope_t(x, cos, sin):
    n = ROPE_HALF
    a, b, c, d = x[:, 0:n], x[:, n:2 * n], x[:, 2 * n:3 * n], x[:, 3 * n:4 * n]
    cr, cc = cos[0:n][None], cos[n:2 * n][None]
    sr, sc = sin[0:n][None], sin[n:2 * n][None]
    return jnp.concatenate(
        [a * cr - b * sr, b * cr + a * sr, c * cc - d * sc, d * cc + c * sc], axis=1)


def _head_rmsnorm_t(x, g):
    r = lax.rsqrt(jnp.mean(x * x, axis=1, keepdims=True) + EPS)
    return x * r * g[None]


def _in_proj_rows(x, r0, preg_ref, wnat_ref, wt_ref, lng_ref, lnb_ref, ws2_ref, sbb_ref,
                  qg_ref, kg_ref, cos_ref, sin_ref, ya_ref, sgb_ref, qt_ref, k_ref, vt_ref, q_scale):
    nr = x.shape[0]
    rows = slice(r0, r0 + nr)
    d_a = lng_ref.shape[1]
    n_q_heads = qt_ref.shape[1] * (qt_ref.shape[4] // Q_BLOCK)
    q_per_kv = n_q_heads // KV_HEADS
    d_b = n_q_heads * HEAD_DIM
    d_kv = KV_HEADS * HEAD_DIM

    r = lax.rsqrt(jnp.mean(x * x, axis=-1, keepdims=True) + EPS)
    h = (x * r * preg_ref[...]).astype(_BF16)

    pt = lax.dot_general(wt_ref[...], h, (((1,), (1,)), ((), ())), preferred_element_type=_F32)
    cos = cos_ref[:, rows]
    sin = sin_ref[:, rows]
    reps = nr // qg_ref.shape[1]
    qg = jnp.concatenate([qg_ref[...]] * reps, axis=1)
    kg = jnp.concatenate([kg_ref[...]] * reps, axis=1)

    qt = pt[0:d_b].reshape(n_q_heads, HEAD_DIM, nr)
    qt = (_rope_t(_head_rmsnorm_t(qt, qg), cos, sin) * q_scale).astype(_BF16)
    for kvh in range(KV_HEADS):
        for jj in range(nr // Q_BLOCK):
            qt_ref[0, kvh, r0 // Q_BLOCK + jj] = jnp.concatenate(
                [qt[kvh * q_per_kv + g][:, jj * Q_BLOCK:(jj + 1) * Q_BLOCK] for g in range(q_per_kv)],
                axis=1)

    kt = pt[d_b:d_b + d_kv].reshape(KV_HEADS, HEAD_DIM, nr)
    kt = _rope_t(_head_rmsnorm_t(kt, kg), cos, sin).reshape(d_kv, nr)
    k_ref[rows, :] = kt.T.astype(_BF16)

    vt = pt[d_b + d_kv:d_b + 2 * d_kv].astype(_BF16)
    row = lax.broadcasted_iota(jnp.int32, (V_ROWS - HEAD_DIM, nr), 0)
    ones_rows = jnp.where(row == 0, 1.0, 0.0).astype(_BF16)
    for kvh in range(KV_HEADS):
        vt_ref[0, kvh, :, rows] = jnp.concatenate([vt[kvh * HEAD_DIM:(kvh + 1) * HEAD_DIM], ones_rows], axis=0)

    pn = jnp.dot(h, wnat_ref[...], preferred_element_type=_F32)
    u = pn[:, 0:d_a]
    v = pn[:, d_a:2 * d_a]
    ga = pn[:, 2 * d_a:3 * d_a]
    gb = pn[:, 3 * d_a:3 * d_a + d_b]
    sgb_ref[rows, :] = _silu(gb)

    mu = jnp.mean(v, axis=-1, keepdims=True)
    vc = v - mu
    var = jnp.mean(vc * vc, axis=-1, keepdims=True)
    vn = (vc * lax.rsqrt(var + EPS) * lng_ref[...] + lnb_ref[...]).astype(_BF16)

    lane = lax.broadcasted_iota(jnp.int32, (CHUNK, 2 * HEAD_DIM), 1)
    first = lane < HEAD_DIM
    zero = jnp.zeros((CHUNK, 2 * HEAD_DIM), _BF16)
    n_chunks = nr // CHUNK
    z_chunks = [[] for _ in range(n_chunks)]
    for c0 in range(0, n_chunks, 2):
        for p in range(d_a // (2 * HEAD_DIM)):
            rhs = []
            for c in (c0, c0 + 1):
                blk = vn[c * CHUNK:(c + 1) * CHUNK, p * 128:(p + 1) * 128]
                rhs.append(jnp.concatenate([jnp.where(first, blk, zero), jnp.where(first, zero, blk)], axis=0))
            zc = jnp.dot(ws2_ref[p], jnp.concatenate(rhs, axis=1), preferred_element_type=_F32)
            z_chunks[c0].append(zc[:, 0:128])
            z_chunks[c0 + 1].append(zc[:, 128:256])
    z = jnp.concatenate([jnp.concatenate(zb, axis=1) + sbb_ref[...] for zb in z_chunks], axis=0)
    ya_ref[rows, :] = (u * z * _silu(ga)).astype(_BF16)


def _residual_rows(x, ya, yb, wo_ref, postg_ref):
    y = jnp.concatenate([ya, yb], axis=1)
    out = jnp.dot(y, wo_ref[...], preferred_element_type=_F32)
    r = lax.rsqrt(jnp.mean(out * out, axis=-1, keepdims=True) + EPS)
    return x + out * r * postg_ref[...]


def _sub_tiles(tm):
    sub = min(SUB_ROWS, tm)
    return [slice(r0, r0 + sub) for r0 in range(0, tm, sub)]


def _in_proj_kernel(x_ref, *refs, q_scale):
    for rows in _sub_tiles(x_ref.shape[0]):
        _in_proj_rows(x_ref[rows, :], rows.start, *refs, q_scale)


def _boundary_kernel(x_ref, ya_ref, yb_ref, wo_ref, postg_ref, *refs, q_scale):
    xo_ref, in_refs = refs[-1], refs[:-1]
    subs = _sub_tiles(x_ref.shape[0])
    xs = []
    for rows in subs:
        xn = _residual_rows(x_ref[rows, :], ya_ref[rows, :], yb_ref[rows, :], wo_ref, postg_ref)
        xo_ref[rows, :] = xn
        xs.append(xn)
    for rows, xn in zip(subs, xs):
        _in_proj_rows(xn, rows.start, *in_refs, q_scale)


def _out_proj_kernel(x_ref, ya_ref, yb_ref, wo_ref, postg_ref, o_ref):
    for rows in _sub_tiles(x_ref.shape[0]):
        o_ref[rows, :] = _residual_rows(x_ref[rows, :], ya_ref[rows, :], yb_ref[rows, :], wo_ref, postg_ref)


def _score_weights(qt, kvh):
    zero = jnp.zeros_like(qt)
    return jnp.concatenate([zero] * kvh + [qt] + [zero] * (KV_HEADS - 1 - kvh), axis=0)


def _score_chunk(k_ref, w, s_ref, m8, c):
    rows = slice(c * KV_CHUNK, (c + 1) * KV_CHUNK)
    s = jnp.dot(k_ref[rows, :], w, preferred_element_type=_F32)
    s_ref[rows, :] = s
    cm = jnp.max(s.reshape(KV_CHUNK // 8, 8, s.shape[1]), axis=0)
    return cm if m8 is None else jnp.maximum(m8, cm)


def _value_chunk(s_ref, m, vt_ref, kvh, acc, c, row0):
    rows = slice(c * KV_CHUNK, (c + 1) * KV_CHUNK)
    srows = pl.ds(pl.multiple_of(row0 + c * KV_CHUNK, KV_CHUNK), KV_CHUNK)
    p = jnp.exp2(s_ref[srows, :] - m).astype(_BF16)
    part = jnp.dot(vt_ref[0, kvh, :, rows], p, preferred_element_type=_F32)
    return part if acc is None else acc + part


def _store_head_output(acc, sgb_ref, yb_ref, kvh, blk):
    d_pair = 2 * HEAD_DIM
    o = acc[0:HEAD_DIM] / acc[HEAD_DIM:HEAD_DIM + 1]
    n_pairs = o.shape[1] // (2 * Q_BLOCK)
    qrows = slice(blk * Q_BLOCK, (blk + 1) * Q_BLOCK)
    for pr in range(n_pairs):
        pair = jnp.concatenate(
            [o[:, (2 * pr) * Q_BLOCK:(2 * pr + 1) * Q_BLOCK],
             o[:, (2 * pr + 1) * Q_BLOCK:(2 * pr + 2) * Q_BLOCK]], axis=0)
        lo = (kvh * n_pairs + pr) * d_pair
        yb_ref[qrows, lo:lo + d_pair] = (pair.T * sgb_ref[qrows, lo:lo + d_pair]).astype(_BF16)


def _scores_only(qt, kvh, k_ref, s_ref, m_ref):
    w = _score_weights(qt, kvh)
    m8 = None
    for c in range(k_ref.shape[0] // KV_CHUNK):
        m8 = _score_chunk(k_ref, w, s_ref, m8, c)
    m_ref[...] = m8


def _values_and_scores(s_r_ref, m_r_ref, kvh, blk, qt, k_ref, s_w_ref, m_w_ref, vt_ref, sgb_ref, yb_ref, row0):
    w = _score_weights(qt, kvh)
    m = jnp.max(m_r_ref[...], axis=0, keepdims=True)
    m8 = None
    acc = None
    for c in range(k_ref.shape[0] // KV_CHUNK):
        acc = _value_chunk(s_r_ref, m, vt_ref, kvh, acc, c, row0)
        m8 = _score_chunk(k_ref, w, s_w_ref, m8, c)
    m_w_ref[...] = m8
    _store_head_output(acc, sgb_ref, yb_ref, kvh, blk)


def _attention_kernel(qt_ref, qt_next_ref, k_ref, k_next_ref, vt_ref, sgb_ref, yb_ref, *scratch):
    s_refs, m_refs = scratch[:S_BUFFERS], scratch[S_BUFFERS:]
    n_units = BLOCKS_PER_STEP * KV_HEADS

    @pl.when((pl.program_id(0) == 0) & (pl.program_id(1) == 0))
    def _():
        for kvh in range(KV_HEADS):
            _scores_only(qt_ref[0, kvh, 0], kvh, k_ref, s_refs[kvh], m_refs[kvh])

    row0 = jnp.minimum(pl.program_id(1), 0)
    for u in range(n_units):
        blk, kvh = divmod(u, KV_HEADS)
        if blk + 1 < BLOCKS_PER_STEP:
            qt_ahead, k_ahead = qt_ref[0, kvh, blk + 1], k_ref
        else:
            qt_ahead, k_ahead = qt_next_ref[0, kvh, 0], k_next_ref
        _values_and_scores(
            s_refs[u % S_BUFFERS], m_refs[u % S_BUFFERS], kvh, blk, qt_ahead, k_ahead,
            s_refs[(u + KV_HEADS) % S_BUFFERS], m_refs[(u + KV_HEADS) % S_BUFFERS],
            vt_ref, sgb_ref, yb_ref, row0)


def _rope_tables_t(seq_len):
    pos = jnp.arange(seq_len, dtype=jnp.int32)
    row = (pos // GRID_W).astype(_F32)
    col = (pos % GRID_W).astype(_F32)
    inv = ROPE_THETA ** (-jnp.arange(0, ROPE_AXIS_DIM, 2, dtype=_F32) / ROPE_AXIS_DIM)
    ang = jnp.concatenate([inv[:, None] * row[None, :], inv[:, None] * col[None, :]], axis=0)
    return jnp.cos(ang), jnp.sin(ang)


def _const_spec(shape):
    return pl.BlockSpec(shape, lambda *_: (0,) * len(shape), pipeline_mode=pl.Buffered(1))


class _Dims:
    def __init__(self, x, w_in, ln_a_g):
        self.batch, self.seq, self.d_model = x.shape
        self.d_a = ln_a_g.shape[1]
        self.d_kv = KV_HEADS * HEAD_DIM
        self.d_b = (w_in.shape[2] - 3 * self.d_a - 2 * self.d_kv) // 2
        self.n_q_heads = self.d_b // HEAD_DIM
        self.q_per_kv = self.n_q_heads // KV_HEADS
        self.rows = self.batch * self.seq
        self.tm = min(ROW_TILE, self.seq)
        self.n_qb = self.seq // Q_BLOCK
        self.qw = self.q_per_kv * Q_BLOCK
        assert self.seq % self.tm == 0 and self.tm % min(SUB_ROWS, self.tm) == 0
        assert min(SUB_ROWS, self.tm) % (2 * CHUNK) == 0 and self.seq % GRID_W == 0
        assert self.d_a == H_A * HEAD_DIM and self.q_per_kv % 2 == 0
        assert self.n_qb % BLOCKS_PER_STEP == 0 and S_BUFFERS == 2 * KV_HEADS
        assert (BLOCKS_PER_STEP * KV_HEADS) % S_BUFFERS == 0


def _make_calls(dm):
    batch, seq, d_model, d_a, d_b, d_kv = dm.batch, dm.seq, dm.d_model, dm.d_a, dm.d_b, dm.d_kv
    rows, tm, n_qb, qw = dm.rows, dm.tm, dm.n_qb, dm.qw
    d_mix = d_a + d_b
    tiles_per_seq = seq // tm
    qb_per_tile = tm // Q_BLOCK
    q_scale = HEAD_DIM ** -0.5 * math.log2(math.e)
    cparams = pltpu.CompilerParams(dimension_semantics=("arbitrary",), vmem_limit_bytes=VMEM_LIMIT_BYTES)

    def row_spec(width):
        return pl.BlockSpec((tm, width), lambda i: (i, 0))

    in_proj_in_specs = [
        _const_spec((1, d_model)),
        _const_spec((d_model, 3 * d_a + d_b)),
        _const_spec((d_b + 2 * d_kv, d_model)),
        _const_spec((1, d_a)),
        _const_spec((1, d_a)),
        _const_spec((H_A // 2, CHUNK, 2 * CHUNK)),
        _const_spec((CHUNK, d_a)),
        _const_spec((HEAD_DIM, 128)),
        _const_spec((HEAD_DIM, 128)),
        pl.BlockSpec((ROPE_AXIS_DIM, tm), lambda i: (0, i % tiles_per_seq)),
        pl.BlockSpec((ROPE_AXIS_DIM, tm), lambda i: (0, i % tiles_per_seq)),
    ]
    in_proj_out_specs = [
        row_spec(d_a),
        row_spec(d_b),
        pl.BlockSpec((1, KV_HEADS, qb_per_tile, HEAD_DIM, qw),
                     lambda i: (i // tiles_per_seq, 0, i % tiles_per_seq, 0, 0)),
        row_spec(d_kv),
        pl.BlockSpec((1, KV_HEADS, V_ROWS, tm), lambda i: (i // tiles_per_seq, 0, 0, i % tiles_per_seq)),
    ]
    in_proj_out_shape = [
        jax.ShapeDtypeStruct((rows, d_a), _BF16),
        jax.ShapeDtypeStruct((rows, d_b), _F32),
        jax.ShapeDtypeStruct((batch, KV_HEADS, n_qb, HEAD_DIM, qw), _BF16),
        jax.ShapeDtypeStruct((rows, d_kv), _BF16),
        jax.ShapeDtypeStruct((batch, KV_HEADS, V_ROWS, seq), _BF16),
    ]
    out_proj_in_specs = [
        row_spec(d_model), row_spec(d_a), row_spec(d_b), _const_spec((d_mix, d_model)), _const_spec((1, d_model))]
    x_shape = jax.ShapeDtypeStruct((rows, d_model), _F32)

    in_proj = pl.pallas_call(
        functools.partial(_in_proj_kernel, q_scale=q_scale),
        grid=(rows // tm,),
        in_specs=[row_spec(d_model)] + in_proj_in_specs,
        out_specs=in_proj_out_specs,
        out_shape=in_proj_out_shape,
        compiler_params=cparams,
        name="in_proj",
    )

    boundary = pl.pallas_call(
        functools.partial(_boundary_kernel, q_scale=q_scale),
        grid=(rows // tm,),
        in_specs=out_proj_in_specs + in_proj_in_specs,
        out_specs=in_proj_out_specs + [row_spec(d_model)],
        out_shape=in_proj_out_shape + [x_shape],
        compiler_params=cparams,
        name="boundary",
    )

    out_proj = pl.pallas_call(
        _out_proj_kernel,
        grid=(rows // tm,),
        in_specs=out_proj_in_specs,
        out_specs=row_spec(d_model),
        out_shape=x_shape,
        compiler_params=cparams,
        name="out_proj",
    )

    steps = n_qb // BLOCKS_PER_STEP
    last = steps - 1

    def next_batch(b, j):
        return jnp.minimum(b + (j == last).astype(jnp.int32), batch - 1)

    def next_block(b, j):
        return jnp.where(j == last, 0, BLOCKS_PER_STEP * (j + 1))

    attention = pl.pallas_call(
        _attention_kernel,
        grid=(batch, steps),
        in_specs=[
            pl.BlockSpec((1, KV_HEADS, BLOCKS_PER_STEP, HEAD_DIM, qw), lambda b, j: (b, 0, j, 0, 0)),
            pl.BlockSpec((1, KV_HEADS, 1, HEAD_DIM, qw), lambda b, j: (next_batch(b, j), 0, next_block(b, j), 0, 0)),
            pl.BlockSpec((seq, d_kv), lambda b, j: (b, 0)),
            pl.BlockSpec((seq, d_kv), lambda b, j: (next_batch(b, j), 0)),
            pl.BlockSpec((1, KV_HEADS, V_ROWS, seq), lambda b, j: (b, 0, 0, 0)),
            pl.BlockSpec((BLOCKS_PER_STEP * Q_BLOCK, d_b), lambda b, j: (b * steps + j, 0)),
        ],
        out_specs=pl.BlockSpec((BLOCKS_PER_STEP * Q_BLOCK, d_b), lambda b, j: (b * steps + j, 0)),
        out_shape=jax.ShapeDtypeStruct((rows, d_b), _BF16),
        scratch_shapes=[pltpu.VMEM((seq, qw), _F32)] * S_BUFFERS + [pltpu.VMEM((8, qw), _F32)] * S_BUFFERS,
        compiler_params=pltpu.CompilerParams(
            dimension_semantics=("arbitrary", "arbitrary"), vmem_limit_bytes=VMEM_LIMIT_BYTES),
        name="attention",
    )
    return in_proj, boundary, attention, out_proj


def _in_proj_operands(dm, w_in, pre_g, ln_g, ln_b, spatial_w, spatial_b, q_g, k_g):
    d_a, d_b, d_kv = dm.d_a, dm.d_b, dm.d_kv
    depth = w_in.shape[0]
    o_q = 3 * d_a
    o_gb = o_q + d_b + 2 * d_kv
    w_nat = jnp.concatenate([w_in[:, :, 0:o_q], w_in[:, :, o_gb:o_gb + d_b]], axis=2).astype(_BF16)
    w_t = jnp.swapaxes(w_in[:, :, o_q:o_gb], 1, 2).astype(_BF16)
    ws = spatial_w.astype(_BF16)
    ws2 = jnp.concatenate([ws[:, 0::2], ws[:, 1::2]], axis=3)
    sbb = jnp.repeat(jnp.swapaxes(spatial_b, 1, 2), HEAD_DIM, axis=2)
    qg = jnp.broadcast_to(q_g[:, :, None], (depth, HEAD_DIM, 128))
    kg = jnp.broadcast_to(k_g[:, :, None], (depth, HEAD_DIM, 128))
    return pre_g[:, None], w_nat, w_t, ln_g[:, None], ln_b[:, None], ws2, sbb, qg, kg


def kernel(x, w_in, w_out, pre_g, post_g, ln_a_g, ln_a_b, spatial_w, spatial_b, q_norm_g, k_norm_g):
    dm = _Dims(x, w_in, ln_a_g)
    in_proj, boundary, attention, out_proj = _make_calls(dm)
    cos_t, sin_t = _rope_tables_t(dm.seq)
    x2 = x.reshape(dm.rows, dm.d_model)
    depth = w_in.shape[0]
    all_ops = _in_proj_operands(dm, w_in, pre_g, ln_a_g, ln_a_b, spatial_w, spatial_b, q_norm_g, k_norm_g)
    w_out = w_out.astype(_BF16)
    post_g = post_g[:, None]
    ya = yb = None
    for l in range(depth):
        ops = [a[l] for a in all_ops]
        if l == 0:
            ya, sgb, qt, k, vt = in_proj(x2, *ops, cos_t, sin_t)
        else:
            ya, sgb, qt, k, vt, x2 = boundary(x2, ya, yb, w_out[l - 1], post_g[l - 1], *ops, cos_t, sin_t)
        yb = attention(qt, qt, k, k, vt, sgb)
    x2 = out_proj(x2, ya, yb, w_out[depth - 1], post_g[depth - 1])
    return x2.reshape(dm.batch, dm.seq, dm.d_model)
```

```python
import functools
import math

import jax
import jax.numpy as jnp
from jax import lax
from jax.experimental import pallas as pl
from jax.experimental.pallas import tpu as pltpu

GRID_W = 64
CHUNK = 128
Q_BLOCK = 128
H_A = 8
HEAD_DIM = 64
KV_HEADS = 2
ROPE_AXIS_DIM = HEAD_DIM // 2
ROPE_HALF = ROPE_AXIS_DIM // 2
ROPE_THETA = 10000.0
EPS = 1e-6

V_ROWS = HEAD_DIM + 16
ROW_TILE = 1024
SUB_ROWS = 512
KV_CHUNK = 256
BLOCKS_PER_STEP = 4
S_BUFFERS = 4
VMEM_LIMIT_BYTES = 56 * 1024 * 1024

_BF16 = jnp.bfloat16
_F32 = jnp.float32


def _silu(g):
    h = 0.5 * g
    return h + h * jnp.tanh(h)


def _rope_t(x, cos, sin):
    n = ROPE_HALF
    a, b, c, d = x[:, 0:n], x[:, n:2 * n], x[:, 2 * n:3 * n], x[:, 3 * n:4 * n]
    cr, cc = cos[0:n][None], cos[n:2 * n][None]
    sr, sc = sin[0:n][None], sin[n:2 * n][None]
    return jnp.concatenate(
        [a * cr - b * sr, b * cr + a * sr, c * cc - d * sc, d * cc + c * sc], axis=1)


def _head_rmsnorm_t(x, g):
    r = lax.rsqrt(jnp.mean(x * x, axis=1, keepdims=True) + EPS)
    return x * r * g[None]


def _in_proj_rows(x, r0, preg_ref, wnat_ref, wt_ref, lng_ref, lnb_ref, ws2_ref, sbb_ref,
                  qg_ref, kg_ref, cos_ref, sin_ref, ya_ref, gb_ref, qt_ref, k_ref, vt_ref, q_scale):
    nr = x.shape[0]
    rows = slice(r0, r0 + nr)
    d_a = lng_ref.shape[1]
    n_q_heads = qt_ref.shape[1] * (qt_ref.shape[4] // Q_BLOCK)
    q_per_kv = n_q_heads // KV_HEADS
    d_b = n_q_heads * HEAD_DIM
    d_kv = KV_HEADS * HEAD_DIM

    r = lax.rsqrt(jnp.mean(x * x, axis=-1, keepdims=True) + EPS)
    h = (x * r * preg_ref[...]).astype(_BF16)

    pt = lax.dot_general(wt_ref[...], h, (((1,), (1,)), ((), ())), preferred_element_type=_F32)
    cos = cos_ref[:, rows]
    sin = sin_ref[:, rows]
    reps = nr // qg_ref.shape[1]
    qg = jnp.concatenate([qg_ref[...]] * reps, axis=1)
    kg = jnp.concatenate([kg_ref[...]] * reps, axis=1)

    qt = pt[0:d_b].reshape(n_q_heads, HEAD_DIM, nr)
    qt = (_rope_t(_head_rmsnorm_t(qt, qg), cos, sin) * q_scale).astype(_BF16)
    for kvh in range(KV_HEADS):
        for jj in range(nr // Q_BLOCK):
            qt_ref[0, kvh, r0 // Q_BLOCK + jj] = jnp.concatenate(
                [qt[kvh * q_per_kv + g][:, jj * Q_BLOCK:(jj + 1) * Q_BLOCK] for g in range(q_per_kv)],
                axis=1)

    kt = pt[d_b:d_b + d_kv].reshape(KV_HEADS, HEAD_DIM, nr)
    kt = _rope_t(_head_rmsnorm_t(kt, kg), cos, sin).reshape(d_kv, nr)
    k_ref[rows, :] = kt.T.astype(_BF16)

    vt = pt[d_b + d_kv:d_b + 2 * d_kv].astype(_BF16)
    row = lax.broadcasted_iota(jnp.int32, (V_ROWS - HEAD_DIM, nr), 0)
    ones_rows = jnp.where(row == 0, 1.0, 0.0).astype(_BF16)
    for kvh in range(KV_HEADS):
        vt_ref[0, kvh, :, rows] = jnp.concatenate([vt[kvh * HEAD_DIM:(kvh + 1) * HEAD_DIM], ones_rows], axis=0)

    pn = jnp.dot(h, wnat_ref[...], preferred_element_type=_F32)
    u = pn[:, 0:d_a]
    v = pn[:, d_a:2 * d_a]
    ga = pn[:, 2 * d_a:3 * d_a]
    gb = pn[:, 3 * d_a:3 * d_a + d_b]
    gb_ref[rows, :] = gb

    mu = jnp.mean(v, axis=-1, keepdims=True)
    vc = v - mu
    var = jnp.mean(vc * vc, axis=-1, keepdims=True)
    vn = (vc * lax.rsqrt(var + EPS) * lng_ref[...] + lnb_ref[...]).astype(_BF16)

    lane = lax.broadcasted_iota(jnp.int32, (CHUNK, 2 * HEAD_DIM), 1)
    first = lane < HEAD_DIM
    zero = jnp.zeros((CHUNK, 2 * HEAD_DIM), _BF16)
    n_chunks = nr // CHUNK
    z_chunks = [[] for _ in range(n_chunks)]
    for c0 in range(0, n_chunks, 2):
        for p in range(d_a // (2 * HEAD_DIM)):
            rhs = []
            for c in (c0, c0 + 1):
                blk = vn[c * CHUNK:(c + 1) * CHUNK, p * 128:(p + 1) * 128]
                rhs.append(jnp.concatenate([jnp.where(first, blk, zero), jnp.where(first, zero, blk)], axis=0))
            zc = jnp.dot(ws2_ref[p], jnp.concatenate(rhs, axis=1), preferred_element_type=_F32)
            z_chunks[c0].append(zc[:, 0:128])
            z_chunks[c0 + 1].append(zc[:, 128:256])
    z = jnp.concatenate([jnp.concatenate(zb, axis=1) + sbb_ref[...] for zb in z_chunks], axis=0)
    ya_ref[rows, :] = (u * z * _silu(ga)).astype(_BF16)


def _residual_rows(x, ya, yb, wo_ref, postg_ref):
    y = jnp.concatenate([ya, yb], axis=1)
    out = jnp.dot(y, wo_ref[...], preferred_element_type=_F32)
    r = lax.rsqrt(jnp.mean(out * out, axis=-1, keepdims=True) + EPS)
    return x + out * r * postg_ref[...]


def _sub_tiles(tm):
    sub = min(SUB_ROWS, tm)
    return [slice(r0, r0 + sub) for r0 in range(0, tm, sub)]


def _in_proj_kernel(x_ref, *refs, q_scale):
    for rows in _sub_tiles(x_ref.shape[0]):
        _in_proj_rows(x_ref[rows, :], rows.start, *refs, q_scale)


def _boundary_kernel(x_ref, ya_ref, yb_ref, wo_ref, postg_ref, *refs, q_scale):
    xo_ref, in_refs = refs[-1], refs[:-1]
    subs = _sub_tiles(x_ref.shape[0])
    xs = []
    for rows in subs:
        xn = _residual_rows(x_ref[rows, :], ya_ref[rows, :], yb_ref[rows, :], wo_ref, postg_ref)
        xo_ref[rows, :] = xn
        xs.append(xn)
    for rows, xn in zip(subs, xs):
        _in_proj_rows(xn, rows.start, *in_refs, q_scale)


def _out_proj_kernel(x_ref, ya_ref, yb_ref, wo_ref, postg_ref, o_ref):
    for rows in _sub_tiles(x_ref.shape[0]):
        o_ref[rows, :] = _residual_rows(x_ref[rows, :], ya_ref[rows, :], yb_ref[rows, :], wo_ref, postg_ref)


def _score_weights(qt, kvh):
    zero = jnp.zeros_like(qt)
    return jnp.concatenate([zero] * kvh + [qt] + [zero] * (KV_HEADS - 1 - kvh), axis=0)


def _score_chunk(k_ref, w, s_ref, m8, c):
    rows = slice(c * KV_CHUNK, (c + 1) * KV_CHUNK)
    s = jnp.dot(k_ref[rows, :], w, preferred_element_type=_F32)
    s_ref[rows, :] = s
    cm = jnp.max(s.reshape(KV_CHUNK // 8, 8, s.shape[1]), axis=0)
    return cm if m8 is None else jnp.maximum(m8, cm)


def _value_chunk(s_ref, m, vt_ref, kvh, acc, c, row0):
    rows = slice(c * KV_CHUNK, (c + 1) * KV_CHUNK)
    srows = pl.ds(pl.multiple_of(row0 + c * KV_CHUNK, KV_CHUNK), KV_CHUNK)
    p = jnp.exp2(s_ref[srows, :] - m).astype(_BF16)
    part = jnp.dot(vt_ref[0, kvh, :, rows], p, preferred_element_type=_F32)
    return part if acc is None else acc + part


def _store_head_output(acc, gb_ref, yb_ref, kvh, blk):
    d_pair = 2 * HEAD_DIM
    o = acc[0:HEAD_DIM] / acc[HEAD_DIM:HEAD_DIM + 1]
    n_pairs = o.shape[1] // (2 * Q_BLOCK)
    qrows = slice(blk * Q_BLOCK, (blk + 1) * Q_BLOCK)
    for pr in range(n_pairs):
        pair = jnp.concatenate(
            [o[:, (2 * pr) * Q_BLOCK:(2 * pr + 1) * Q_BLOCK],
             o[:, (2 * pr + 1) * Q_BLOCK:(2 * pr + 2) * Q_BLOCK]], axis=0)
        lo = (kvh * n_pairs + pr) * d_pair
        yb_ref[qrows, lo:lo + d_pair] = (pair.T * _silu(gb_ref[qrows, lo:lo + d_pair])).astype(_BF16)


def _scores_only(qt, kvh, k_ref, s_ref, m_ref):
    w = _score_weights(qt, kvh)
    m8 = None
    for c in range(k_ref.shape[0] // KV_CHUNK):
        m8 = _score_chunk(k_ref, w, s_ref, m8, c)
    m_ref[...] = m8


def _values_and_scores(s_r_ref, m_r_ref, kvh, blk, qt, k_ref, s_w_ref, m_w_ref, vt_ref, gb_ref, yb_ref, row0):
    w = _score_weights(qt, kvh)
    m = jnp.max(m_r_ref[...], axis=0, keepdims=True)
    m8 = None
    acc = None
    for c in range(k_ref.shape[0] // KV_CHUNK):
        acc = _value_chunk(s_r_ref, m, vt_ref, kvh, acc, c, row0)
        m8 = _score_chunk(k_ref, w, s_w_ref, m8, c)
    m_w_ref[...] = m8
    _store_head_output(acc, gb_ref, yb_ref, kvh, blk)


def _attention_kernel(qt_ref, qt_next_ref, k_ref, k_next_ref, vt_ref, gb_ref, yb_ref, *scratch):
    s_refs, m_refs = scratch[:S_BUFFERS], scratch[S_BUFFERS:]
    n_units = BLOCKS_PER_STEP * KV_HEADS

    @pl.when((pl.program_id(0) == 0) & (pl.program_id(1) == 0))
    def _():
        for kvh in range(KV_HEADS):
            _scores_only(qt_ref[0, kvh, 0], kvh, k_ref, s_refs[kvh], m_refs[kvh])

    row0 = jnp.minimum(pl.program_id(1), 0)
    for u in range(n_units):
        blk, kvh = divmod(u, KV_HEADS)
        if blk + 1 < BLOCKS_PER_STEP:
            qt_ahead, k_ahead = qt_ref[0, kvh, blk + 1], k_ref
        else:
            qt_ahead, k_ahead = qt_next_ref[0, kvh, 0], k_next_ref
        _values_and_scores(
            s_refs[u % S_BUFFERS], m_refs[u % S_BUFFERS], kvh, blk, qt_ahead, k_ahead,
            s_refs[(u + KV_HEADS) % S_BUFFERS], m_refs[(u + KV_HEADS) % S_BUFFERS],
            vt_ref, gb_ref, yb_ref, row0)


def _rope_tables_t(seq_len):
    pos = jnp.arange(seq_len, dtype=jnp.int32)
    row = (pos // GRID_W).astype(_F32)
    col = (pos % GRID_W).astype(_F32)
    inv = ROPE_THETA ** (-jnp.arange(0, ROPE_AXIS_DIM, 2, dtype=_F32) / ROPE_AXIS_DIM)
    ang = jnp.concatenate([inv[:, None] * row[None, :], inv[:, None] * col[None, :]], axis=0)
    return jnp.cos(ang), jnp.sin(ang)


def _const_spec(shape):
    return pl.BlockSpec(shape, lambda *_: (0,) * len(shape), pipeline_mode=pl.Buffered(1))


class _Dims:
    def __init__(self, x, w_in, ln_a_g):
        self.batch, self.seq, self.d_model = x.shape
        self.d_a = ln_a_g.shape[1]
        self.d_kv = KV_HEADS * HEAD_DIM
        self.d_b = (w_in.shape[2] - 3 * self.d_a - 2 * self.d_kv) // 2
        self.n_q_heads = self.d_b // HEAD_DIM
        self.q_per_kv = self.n_q_heads // KV_HEADS
        self.rows = self.batch * self.seq
        self.tm = min(ROW_TILE, self.seq)
        self.n_qb = self.seq // Q_BLOCK
        self.qw = self.q_per_kv * Q_BLOCK
        assert self.seq % self.tm == 0 and self.tm % min(SUB_ROWS, self.tm) == 0
        assert min(SUB_ROWS, self.tm) % (2 * CHUNK) == 0 and self.seq % GRID_W == 0
        assert self.d_a == H_A * HEAD_DIM and self.q_per_kv % 2 == 0
        assert self.n_qb % BLOCKS_PER_STEP == 0 and S_BUFFERS == 2 * KV_HEADS
        assert (BLOCKS_PER_STEP * KV_HEADS) % S_BUFFERS == 0


def _make_calls(dm):
    batch, seq, d_model, d_a, d_b, d_kv = dm.batch, dm.seq, dm.d_model, dm.d_a, dm.d_b, dm.d_kv
    rows, tm, n_qb, qw = dm.rows, dm.tm, dm.n_qb, dm.qw
    d_mix = d_a + d_b
    tiles_per_seq = seq // tm
    qb_per_tile = tm // Q_BLOCK
    q_scale = HEAD_DIM ** -0.5 * math.log2(math.e)
    cparams = pltpu.CompilerParams(dimension_semantics=("arbitrary",), vmem_limit_bytes=VMEM_LIMIT_BYTES)

    def row_spec(width):
        return pl.BlockSpec((tm, width), lambda i: (i, 0))

    in_proj_in_specs = [
        _const_spec((1, d_model)),
        _const_spec((d_model, 3 * d_a + d_b)),
        _const_spec((d_b + 2 * d_kv, d_model)),
        _const_spec((1, d_a)),
        _const_spec((1, d_a)),
        _const_spec((H_A // 2, CHUNK, 2 * CHUNK)),
        _const_spec((CHUNK, d_a)),
        _const_spec((HEAD_DIM, 128)),
        _const_spec((HEAD_DIM, 128)),
        pl.BlockSpec((ROPE_AXIS_DIM, tm), lambda i: (0, i % tiles_per_seq)),
        pl.BlockSpec((ROPE_AXIS_DIM, tm), lambda i: (0, i % tiles_per_seq)),
    ]
    in_proj_out_specs = [
        row_spec(d_a),
        row_spec(d_b),
        pl.BlockSpec((1, KV_HEADS, qb_per_tile, HEAD_DIM, qw),
                     lambda i: (i // tiles_per_seq, 0, i % tiles_per_seq, 0, 0)),
        row_spec(d_kv),
        pl.BlockSpec((1, KV_HEADS, V_ROWS, tm), lambda i: (i // tiles_per_seq, 0, 0, i % tiles_per_seq)),
    ]
    in_proj_out_shape = [
        jax.ShapeDtypeStruct((rows, d_a), _BF16),
        jax.ShapeDtypeStruct((rows, d_b), _F32),
        jax.ShapeDtypeStruct((batch, KV_HEADS, n_qb, HEAD_DIM, qw), _BF16),
        jax.ShapeDtypeStruct((rows, d_kv), _BF16),
        jax.ShapeDtypeStruct((batch, KV_HEADS, V_ROWS, seq), _BF16),
    ]
    out_proj_in_specs = [
        row_spec(d_model), row_spec(d_a), row_spec(d_b), _const_spec((d_mix, d_model)), _const_spec((1, d_model))]
    x_shape = jax.ShapeDtypeStruct((rows, d_model), _F32)

    in_proj = pl.pallas_call(
        functools.partial(_in_proj_kernel, q_scale=q_scale),
        grid=(rows // tm,),
        in_specs=[row_spec(d_model)] + in_proj_in_specs,
        out_specs=in_proj_out_specs,
        out_shape=in_proj_out_shape,
        compiler_params=cparams,
        name="in_proj",
    )

    boundary = pl.pallas_call(
        functools.partial(_boundary_kernel, q_scale=q_scale),
        grid=(rows // tm,),
        in_specs=out_proj_in_specs + in_proj_in_specs,
        out_specs=in_proj_out_specs + [row_spec(d_model)],
        out_shape=in_proj_out_shape + [x_shape],
        compiler_params=cparams,
        name="boundary",
    )

    out_proj = pl.pallas_call(
        _out_proj_kernel,
        grid=(rows // tm,),
        in_specs=out_proj_in_specs,
        out_specs=row_spec(d_model),
        out_shape=x_shape,
        compiler_params=cparams,
        name="out_proj",
    )

    steps = n_qb // BLOCKS_PER_STEP
    last = steps - 1

    def next_batch(b, j):
        return jnp.minimum(b + (j == last).astype(jnp.int32), batch - 1)

    def next_block(b, j):
        return jnp.where(j == last, 0, BLOCKS_PER_STEP * (j + 1))

    attention = pl.pallas_call(
        _attention_kernel,
        grid=(batch, steps),
        in_specs=[
            pl.BlockSpec((1, KV_HEADS, BLOCKS_PER_STEP, HEAD_DIM, qw), lambda b, j: (b, 0, j, 0, 0)),
            pl.BlockSpec((1, KV_HEADS, 1, HEAD_DIM, qw), lambda b, j: (next_batch(b, j), 0, next_block(b, j), 0, 0)),
            pl.BlockSpec((seq, d_kv), lambda b, j: (b, 0)),
            pl.BlockSpec((seq, d_kv), lambda b, j: (next_batch(b, j), 0)),
            pl.BlockSpec((1, KV_HEADS, V_ROWS, seq), lambda b, j: (b, 0, 0, 0)),
            pl.BlockSpec((BLOCKS_PER_STEP * Q_BLOCK, d_b), lambda b, j: (b * steps + j, 0)),
        ],
        out_specs=pl.BlockSpec((BLOCKS_PER_STEP * Q_BLOCK, d_b), lambda b, j: (b * steps + j, 0)),
        out_shape=jax.ShapeDtypeStruct((rows, d_b), _BF16),
        scratch_shapes=[pltpu.VMEM((seq, qw), _F32)] * S_BUFFERS + [pltpu.VMEM((8, qw), _F32)] * S_BUFFERS,
        compiler_params=pltpu.CompilerParams(
            dimension_semantics=("arbitrary", "arbitrary"), vmem_limit_bytes=VMEM_LIMIT_BYTES),
        name="attention",
    )
    return in_proj, boundary, attention, out_proj


def _in_proj_operands(dm, w_in, pre_g, ln_g, ln_b, spatial_w, spatial_b, q_g, k_g):
    d_a, d_b, d_kv = dm.d_a, dm.d_b, dm.d_kv
    depth = w_in.shape[0]
    o_q = 3 * d_a
    o_gb = o_q + d_b + 2 * d_kv
    w_nat = jnp.concatenate([w_in[:, :, 0:o_q], w_in[:, :, o_gb:o_gb + d_b]], axis=2).astype(_BF16)
    w_t = jnp.swapaxes(w_in[:, :, o_q:o_gb], 1, 2).astype(_BF16)
    ws = spatial_w.astype(_BF16)
    ws2 = jnp.concatenate([ws[:, 0::2], ws[:, 1::2]], axis=3)
    sbb = jnp.repeat(jnp.swapaxes(spatial_b, 1, 2), HEAD_DIM, axis=2)
    qg = jnp.broadcast_to(q_g[:, :, None], (depth, HEAD_DIM, 128))
    kg = jnp.broadcast_to(k_g[:, :, None], (depth, HEAD_DIM, 128))
    return pre_g[:, None], w_nat, w_t, ln_g[:, None], ln_b[:, None], ws2, sbb, qg, kg


def kernel(x, w_in, w_out, pre_g, post_g, ln_a_g, ln_a_b, spatial_w, spatial_b, q_norm_g, k_norm_g):
    dm = _Dims(x, w_in, ln_a_g)
    in_proj, boundary, attention, out_proj = _make_calls(dm)
    cos_t, sin_t = _rope_tables_t(dm.seq)
    x2 = x.reshape(dm.rows, dm.d_model)
    depth = w_in.shape[0]
    all_ops = _in_proj_operands(dm, w_in, pre_g, ln_a_g, ln_a_b, spatial_w, spatial_b, q_norm_g, k_norm_g)
    w_out = w_out.astype(_BF16)
    post_g = post_g[:, None]
    ya = yb = None
    for l in range(depth):
        ops = [a[l] for a in all_ops]
        if l == 0:
            ya, gb, qt, k, vt = in_proj(x2, *ops, cos_t, sin_t)
        else:
            ya, gb, qt, k, vt, x2 = boundary(x2, ya, yb, w_out[l - 1], post_g[l - 1], *ops, cos_t, sin_t)
        yb = attention(qt, qt, k, k, vt, gb)
    x2 = out_proj(x2, ya, yb, w_out[depth - 1], post_g[depth - 1])
    return x2.reshape(dm.batch, dm.seq, dm.d_model)
```
